```python
import math
import jax, jax.numpy as jnp
from jax import lax
import numpy as np

D_MODEL = 2048
BATCH = 8
SEQ = 2048
DEPTH = 1

CONV_CH = D_MODEL // 2
ATTN_WIDTH = D_MODEL - CONV_CH
HEAD_DIM = 128
N_HEADS = ATTN_WIDTH // HEAD_DIM
CONV_GROUPS = CONV_CH // HEAD_DIM
CONV_WIDTH = 31
DILATED_PATTERNS = ((128, 1), (512, 4), (2048, 16))
FFN_DIM = 5632
FFN_CONV_WIDTH = 3
IN_COLS = 2 * CONV_CH + 3 * ATTN_WIDTH
RMS_EPS = 1e-6
LN_EPS = 1e-5
NEG_BIG = -1e30

kernel_name = "hybrid_conformer_dilated_attn_encoder"


def rmsnorm(x, g):
    xf = x.astype(jnp.float32)
    y = xf * lax.rsqrt(jnp.mean(xf * xf, axis=-1, keepdims=True) + RMS_EPS)
    return (y * g.astype(jnp.float32)).astype(x.dtype)


def layernorm(x, g, b):
    xf = x.astype(jnp.float32)
    mu = jnp.mean(xf, axis=-1, keepdims=True)
    xc = xf - mu
    var = jnp.mean(xc * xc, axis=-1, keepdims=True)
    y = xc * lax.rsqrt(var + LN_EPS)
    return (y * g.astype(jnp.float32) + b.astype(jnp.float32)).astype(x.dtype)


def depthwise_conv_seq(x, w, b):
    k = w.shape[0]
    c = x.shape[-1]
    half = (k - 1) // 2
    y = lax.conv_general_dilated(
        x, w[:, None, :].astype(x.dtype), window_strides=(1,),
        padding=[(half, k - 1 - half)],
        dimension_numbers=("NWC", "WIO", "NWC"),
        feature_group_count=c)
    return y + b.astype(x.dtype)


def alibi_slopes(n_heads):
    return jnp.asarray([2.0 ** (-8.0 * (i + 1) / n_heads) for i in range(n_heads)], jnp.float32)


def dilated_branch(q, k, v, slopes, window, dilation):
    bsz, nh, s_len, hd = q.shape
    r = window // (2 * dilation)
    sub_len = s_len // dilation
    nb = -(-sub_len // r)
    lp = nb * r

    def to_sub(t):
        return t.reshape(bsz, nh, sub_len, dilation, hd).transpose(0, 1, 3, 2, 4)

    qs = jnp.pad(to_sub(q), ((0, 0), (0, 0), (0, 0), (0, lp - sub_len), (0, 0)))
    qs = qs.reshape(bsz, nh, dilation, nb, r, hd)
    pad_kv = ((0, 0), (0, 0), (0, 0), (r, lp - sub_len + r), (0, 0))

    def kv_blocks(t):
        tp = jnp.pad(to_sub(t), pad_kv).reshape(bsz, nh, dilation, nb + 2, r, hd)
        return jnp.concatenate(
            [tp[:, :, :, 0:nb], tp[:, :, :, 1:nb + 1], tp[:, :, :, 2:nb + 2]], axis=4)

    kb = kv_blocks(k)
    vb = kv_blocks(v)

    p_idx = np.arange(r)[:, None]
    c_idx = np.arange(3 * r)[None, :]
    off = c_idx - r - p_idx
    a_k = np.arange(nb)[:, None, None] * r + c_idx[None] - r
    valid = (np.abs(off)[None] <= r) & (a_k >= 0) & (a_k < sub_len)
    dist = jnp.asarray(np.abs(off) * dilation, jnp.float32)

    s = jnp.einsum("bhrnqd,bhrnkd->bhrnqk", qs, kb)
    s = s - slopes[None, :, None, None, None, None] * dist
    s = jnp.where(jnp.asarray(valid), s, NEG_BIG)
    m = jnp.max(s, axis=-1, keepdims=True)
    pexp = jnp.exp(s - m)
    den = jnp.sum(pexp, axis=-1, keepdims=True)
    o = jnp.einsum("bhrnqk,bhrnkd->bhrnqd", pexp, vb) / den
    lse = (m + jnp.log(den))[..., 0]

    o = o.reshape(bsz, nh, dilation, lp, hd)[:, :, :, :sub_len]
    o = o.transpose(0, 1, 3, 2, 4).reshape(bsz, nh, s_len, hd)
    lse = lse.reshape(bsz, nh, dilation, lp)[:, :, :, :sub_len]
    lse = lse.transpose(0, 1, 3, 2).reshape(bsz, nh, s_len)
    return o, lse


def conformer_conv_group(u_a, conv_dw_w, conv_dw_b, conv_ln_g, conv_ln_b):
    val = u_a[..., :CONV_CH]
    gate = u_a[..., CONV_CH:]
    a = val * jax.nn.sigmoid(gate)
    a = depthwise_conv_seq(a, conv_dw_w, conv_dw_b)
    a = layernorm(a, conv_ln_g, conv_ln_b)
    return jax.nn.silu(a)


def dilated_attention_group(u_b, q_norm_g, k_norm_g):
    bsz, s_len, _ = u_b.shape
    q = u_b[..., :ATTN_WIDTH].reshape(bsz, s_len, N_HEADS, HEAD_DIM)
    k = u_b[..., ATTN_WIDTH:2 * ATTN_WIDTH].reshape(bsz, s_len, N_HEADS, HEAD_DIM)
    v = u_b[..., 2 * ATTN_WIDTH:].reshape(bsz, s_len, N_HEADS, HEAD_DIM)
    q = rmsnorm(q, q_norm_g).astype(jnp.float32) * (HEAD_DIM ** -0.5)
    k = rmsnorm(k, k_norm_g).astype(jnp.float32)
    v = v.astype(jnp.float32)
    q, k, v = (t.transpose(0, 2, 1, 3) for t in (q, k, v))
    slopes = alibi_slopes(N_HEADS)
    outs = []
    lses = []
    for window, dilation in DILATED_PATTERNS:
        o, lse = dilated_branch(q, k, v, slopes, window, dilation)
        outs.append(o)
        lses.append(lse)
    wts = jax.nn.softmax(jnp.stack(lses, axis=0), axis=0)
    o = jnp.sum(wts[..., None] * jnp.stack(outs, axis=0), axis=0)
    return o.transpose(0, 2, 1, 3).reshape(bsz, s_len, ATTN_WIDTH)


def conv_ffn(h, w_up, ffn_dw_w, ffn_dw_b, w_down):
    u = h @ w_up
    u = depthwise_conv_seq(u, ffn_dw_w, ffn_dw_b)
    g = u[..., :FFN_DIM]
    val = u[..., FFN_DIM:]
    return (jax.nn.silu(g) * val) @ w_down


def setup_inputs(seed: int = 0) -> dict:
    key = jax.random.key(seed)
    ks = jax.random.split(key, 16)
    f32 = jnp.float32
    nrm = lambda k, shape, scale: jax.random.normal(k, shape, f32) * scale
    return {
        "x": nrm(ks[0], (BATCH, SEQ, D_MODEL), 1.0),
        "norm1_g": 1.0 + nrm(ks[1], (D_MODEL,), 0.02),
        "w_in": nrm(ks[2], (D_MODEL, IN_COLS), D_MODEL ** -0.5),
        "conv_dw_w": nrm(ks[3], (CONV_WIDTH, CONV_CH), CONV_WIDTH ** -0.5),
        "conv_dw_b": nrm(ks[4], (CONV_CH,), 0.01),
        "conv_ln_g": 1.0 + nrm(ks[5], (CONV_CH,), 0.02),
        "conv_ln_b": nrm(ks[6], (CONV_CH,), 0.01),
        "q_norm_g": 1.0 + nrm(ks[7], (HEAD_DIM,), 0.02),
        "k_norm_g": 1.0 + nrm(ks[8], (HEAD_DIM,), 0.02),
        "w_out": nrm(ks[9], (D_MODEL, D_MODEL), D_MODEL ** -0.5),
        "norm2_g": 1.0 + nrm(ks[10], (D_MODEL,), 0.02),
        "w_up": nrm(ks[11], (D_MODEL, 2 * FFN_DIM), D_MODEL ** -0.5),
        "ffn_dw_w": nrm(ks[12], (FFN_CONV_WIDTH, 2 * FFN_DIM), FFN_CONV_WIDTH ** -0.5),
        "ffn_dw_b": nrm(ks[13], (2 * FFN_DIM,), 0.01),
        "w_down": nrm(ks[14], (FFN_DIM, D_MODEL), FFN_DIM ** -0.5),
    }


def reference(x, norm1_g, w_in, conv_dw_w, conv_dw_b, conv_ln_g, conv_ln_b,
              q_norm_g, k_norm_g, w_out, norm2_g, w_up, ffn_dw_w, ffn_dw_b, w_down):
    for _ in range(DEPTH):
        h = rmsnorm(x, norm1_g)
        u = h @ w_in
        a_out = conformer_conv_group(u[..., :2 * CONV_CH], conv_dw_w, conv_dw_b,
                                     conv_ln_g, conv_ln_b)
        b_out = dilated_attention_group(u[..., 2 * CONV_CH:], q_norm_g, k_norm_g)
        mixed = jnp.concatenate([a_out, b_out.astype(x.dtype)], axis=-1)
        x = x + mixed @ w_out
        h2 = rmsnorm(x, norm2_g)
        x = x + conv_ffn(h2, w_up, ffn_dw_w, ffn_dw_b, w_down)
    return x
```

```python
import functools

import jax
import jax.numpy as jnp
from jax import lax
from jax.experimental import pallas as pl
from jax.experimental.pallas import tpu as pltpu

D_MODEL = 2048
CONV_CH = 1024
ATTN_WIDTH = 1024
HEAD_DIM = 128
N_HEADS = ATTN_WIDTH // HEAD_DIM
CONV_WIDTH = 31
CONV_HALF = (CONV_WIDTH - 1) // 2
FFN_DIM = 5632
IN_COLS = 2 * CONV_CH + 3 * ATTN_WIDTH
RMS_EPS = 1e-6
LN_EPS = 1e-5
NEG_BIG = -1e30
BAND_R = 64
DILATIONS = (1, 4, 16)

LANES = 128
BF16_ROWS = 16
VMEM_LIMIT = 56 * 1024 * 1024

F32 = jnp.float32
BF16 = jnp.bfloat16


def _cparams(sem):
    return pltpu.CompilerParams(dimension_semantics=sem, vmem_limit_bytes=VMEM_LIMIT)


def _in_proj_kernel(x_ref, g_ref, w_ref, o_ref, h_scr):
    @pl.when(pl.program_id(1) == 0)
    def _():
        xf = x_ref[...]
        ms = jnp.mean(xf * xf, axis=-1, keepdims=True)
        h_scr[...] = (xf * lax.rsqrt(ms + RMS_EPS) * g_ref[...]).astype(BF16)

    o_ref[...] = jnp.dot(h_scr[...], w_ref[...], preferred_element_type=F32)


def _in_proj(x2, g, w_bf, tm=512, tn=512):
    m, d = x2.shape
    n = w_bf.shape[1]
    return pl.pallas_call(
        _in_proj_kernel,
        grid=(m // tm, n // tn),
        in_specs=[
            pl.BlockSpec((tm, d), lambda i, j: (i, 0)),
            pl.BlockSpec((1, d), lambda i, j: (0, 0)),
            pl.BlockSpec((d, tn), lambda i, j: (0, j)),
        ],
        out_specs=pl.BlockSpec((tm, tn), lambda i, j: (i, j)),
        out_shape=jax.ShapeDtypeStruct((m, n), F32),
        scratch_shapes=[pltpu.VMEM((tm, d), BF16)],
        compiler_params=_cparams(("parallel", "arbitrary")),
        name="in_proj",
    )(x2, g.reshape(1, d), w_bf)


CONV_TT = 256
CONV_HALO = 16
CONV_RC = 64
CONV_LB = CONV_CH // LANES


def _conv_kernel(val_ref, gate_ref, pval_ref, pgate_ref, nval_ref, ngate_ref,
                 w_ref, b_ref, lg_ref, lb_ref, o_ref, a_scr, y_scr):
    ti = pl.program_id(1)
    nt = pl.num_programs(1)
    tt = CONV_TT

    def glu(v, g):
        return v * jax.nn.sigmoid(g)

    keep_prev = (ti > 0).astype(F32)
    keep_next = (ti < nt - 1).astype(F32)
    for lb in range(CONV_LB):
        ls = slice(lb * LANES, (lb + 1) * LANES)
        a_scr[lb, 0:CONV_HALO, :] = glu(pval_ref[0, :, ls], pgate_ref[0, :, ls]) * keep_prev
        a_scr[lb, CONV_HALO:CONV_HALO + tt, :] = glu(val_ref[0, :, ls], gate_ref[0, :, ls])
        a_scr[lb, CONV_HALO + tt:, :] = glu(nval_ref[0, :, ls], ngate_ref[0, :, ls]) * keep_next

    def lane_block(lb, carry):
        for rc in range(tt // CONV_RC):
            r0 = rc * CONV_RC + CONV_HALO - CONV_HALF
            acc = jnp.broadcast_to(b_ref[lb], (CONV_RC, LANES))
            for k in range(CONV_WIDTH):
                acc = acc + a_scr[lb, r0 + k:r0 + k + CONV_RC, :] * w_ref[lb, k:k + 1, :]
            y_scr[lb, rc * CONV_RC:(rc + 1) * CONV_RC, :] = acc
        return carry

    lax.fori_loop(0, CONV_LB, lane_block, 0)

    tot = y_scr[0]
    for lb in range(1, CONV_LB):
        tot = tot + y_scr[lb]
    mu = jnp.sum(tot, axis=-1, keepdims=True) * (1.0 / CONV_CH)
    sq = None
    for lb in range(CONV_LB):
        c = y_scr[lb] - mu
        sq = c * c if sq is None else sq + c * c
    var = jnp.sum(sq, axis=-1, keepdims=True) * (1.0 / CONV_CH)
    rstd = lax.rsqrt(var + LN_EPS)
    for lb in range(CONV_LB):
        ls = slice(lb * LANES, (lb + 1) * LANES)
        z = (y_scr[lb] - mu) * rstd * lg_ref[:, ls] + lb_ref[:, ls]
        o_ref[0, :, ls] = (z * jax.nn.sigmoid(z)).astype(o_ref.dtype)


def _conv_group(u3, conv_dw_w, conv_dw_b, conv_ln_g, conv_ln_b):
    b, s, _ = u3.shape
    tt, halo = CONV_TT, CONV_HALO
    nt = s // tt
    hb = tt // halo
    n_hblk = s // halo
    w3 = conv_dw_w.reshape(CONV_WIDTH, CONV_LB, LANES).transpose(1, 0, 2)
    b3 = conv_dw_b.reshape(CONV_LB, 1, LANES)
    main = lambda col: pl.BlockSpec((1, tt, CONV_CH), lambda bi, ti: (bi, ti, col))
    prev = lambda col: pl.BlockSpec(
        (1, halo, CONV_CH), lambda bi, ti: (bi, jnp.maximum(ti * hb - 1, 0), col))
    nxt = lambda col: pl.BlockSpec(
        (1, halo, CONV_CH), lambda bi, ti: (bi, jnp.minimum((ti + 1) * hb, n_hblk - 1), col))
    full = lambda shape: pl.BlockSpec(shape, lambda bi, ti: (0,) * len(shape))
    return pl.pallas_call(
        _conv_kernel,
        grid=(b, nt),
        in_specs=[main(0), main(1), prev(0), prev(1), nxt(0), nxt(1),
                  full((CONV_LB, CONV_WIDTH, LANES)), full((CONV_LB, 1, LANES)),
                  full((1, CONV_CH)), full((1, CONV_CH))],
        out_specs=pl.BlockSpec((1, tt, CONV_CH), lambda bi, ti: (bi, ti, 0)),
        out_shape=jax.ShapeDtypeStruct((b, s, CONV_CH), BF16),
        scratch_shapes=[pltpu.VMEM((CONV_LB, tt + 2 * halo, LANES), F32),
                        pltpu.VMEM((CONV_LB, tt, LANES), F32)],
        compiler_params=_cparams(("parallel", "arbitrary")),
        name="conv_group",
    )(u3, u3, u3, u3, u3, u3, w3, b3, conv_ln_g.reshape(1, CONV_CH), conv_ln_b.reshape(1, CONV_CH))


ATT_QB = 128
ATT_KB = ATT_QB + 2 * BAND_R


def _attn_kernel(slope_ref, q_ref, k_ref, v_ref, qg_ref, kg_ref, o_ref,
                 qn, kn, bias_scr, acc_scr, m_scr, l_scr):
    h = pl.program_id(1)
    s_len = q_ref.shape[1]
    slope = slope_ref[h]

    def headnorm(t, g, scale):
        ms = jnp.mean(t * t, axis=-1, keepdims=True)
        return t * lax.rsqrt(ms + RMS_EPS) * g * scale

    qn[...] = headnorm(q_ref[0], qg_ref[...], HEAD_DIM ** -0.5)
    kn[...] = headnorm(k_ref[0], kg_ref[...], 1.0)

    rows = lax.broadcasted_iota(jnp.int32, (ATT_QB, ATT_KB), 0)
    cols = lax.broadcasted_iota(jnp.int32, (ATT_QB, ATT_KB), 1)
    for w, dil in enumerate(DILATIONS):
        for e in range(3):
            off = jnp.abs(cols - rows - e * BAND_R)
            bias = jnp.where(off <= BAND_R, -(slope * dil) * off.astype(F32), NEG_BIG)
            bias_scr[3 * w + e] = bias

    def block(w, q_idx, k_idx, nk, bias):
        qb = qn[q_idx, :].astype(BF16)
        kb = kn[k_idx, :].astype(BF16)
        vb = v_ref[0, k_idx, :].astype(BF16)
        sc = lax.dot_general(qb, kb, (((1,), (1,)), ((), ())), preferred_element_type=F32)
        sc = sc + bias
        m = jnp.max(sc, axis=-1, keepdims=True)
        p = jnp.exp(sc - m)
        l = jnp.sum(p, axis=-1, keepdims=True)
        acc = jnp.dot(p.astype(BF16), vb, preferred_element_type=F32)
        acc_scr[w, q_idx, :] = acc
        m_scr[w, q_idx, :] = jnp.broadcast_to(m, (ATT_QB, LANES))
        l_scr[w, q_idx, :] = jnp.broadcast_to(l, (ATT_QB, LANES))

    for w, dil in enumerate(DILATIONS):
        class_len = s_len // dil
        nqb = class_len // ATT_QB
        nk = min(ATT_KB, class_len)

        def body(it, carry, w=w, dil=dil, class_len=class_len, nqb=nqb, nk=nk):
            c = it // nqb
            i = it % nqb
            q0 = i * ATT_QB
            k0 = jnp.clip(q0 - BAND_R, 0, class_len - nk)
            e = (q0 - k0) // BAND_R
            if dil == 1:
                q_idx = pl.ds(pl.multiple_of(q0, ATT_QB), ATT_QB)
                k_idx = pl.ds(pl.multiple_of(k0, BAND_R), nk)
            else:
                q_idx = pl.ds(c + dil * q0, ATT_QB, stride=dil)
                k_idx = pl.ds(c + dil * k0, nk, stride=dil)
            bias = bias_scr[3 * w + e]
            if nk < ATT_KB:
                bias = bias[:, :nk]
            block(w, q_idx, k_idx, nk, bias)
            return carry

        lax.fori_loop(0, dil * nqb, body, 0)

    chunk = 256
    for r in range(s_len // chunk):
        rs = slice(r * chunk, (r + 1) * chunk)
        m0, m1, m2 = m_scr[0, rs, :], m_scr[1, rs, :], m_scr[2, rs, :]
        mm = jnp.maximum(jnp.maximum(m0, m1), m2)
        a0, a1, a2 = jnp.exp(m0 - mm), jnp.exp(m1 - mm), jnp.exp(m2 - mm)
        num = a0 * acc_scr[0, rs, :] + a1 * acc_scr[1, rs, :] + a2 * acc_scr[2, rs, :]
        den = a0 * l_scr[0, rs, :] + a1 * l_scr[1, rs, :] + a2 * l_scr[2, rs, :]
        o_ref[0, rs, :] = (num / den).astype(o_ref.dtype)


def _attn_group(u3, q_norm_g, k_norm_g):
    b, s, _ = u3.shape
    qcol = 2 * CONV_CH // HEAD_DIM
    slopes = jnp.asarray([2.0 ** (-8.0 * (i + 1) / N_HEADS) for i in range(N_HEADS)], F32)
    head = lambda base: pl.BlockSpec((1, s, HEAD_DIM), lambda bi, hi: (bi, 0, base + hi))
    gain = pl.BlockSpec((1, HEAD_DIM), lambda bi, hi: (0, 0))
    return pl.pallas_call(
        _attn_kernel,
        grid=(b, N_HEADS),
        in_specs=[pl.BlockSpec(memory_space=pltpu.SMEM),
                  head(qcol), head(qcol + N_HEADS), head(qcol + 2 * N_HEADS), gain, gain],
        out_specs=pl.BlockSpec((1, s, HEAD_DIM), lambda bi, hi: (bi, 0, hi)),
        out_shape=jax.ShapeDtypeStruct((b, s, ATTN_WIDTH), BF16),
        scratch_shapes=[pltpu.VMEM((s, HEAD_DIM), F32),
                        pltpu.VMEM((s, HEAD_DIM), F32),
                        pltpu.VMEM((9, ATT_QB, ATT_KB), F32),
                        pltpu.VMEM((3, s, HEAD_DIM), F32),
                        pltpu.VMEM((3, s, LANES), F32),
                        pltpu.VMEM((3, s, LANES), F32)],
        compiler_params=_cparams(("parallel", "arbitrary")),
        name="attn_group",
    )(slopes, u3, u3, u3, q_norm_g.reshape(1, HEAD_DIM), k_norm_g.reshape(1, HEAD_DIM))


def _out_proj_kernel(x_ref, a_ref, b_ref, w_ref, o_ref):
    acc = jnp.dot(a_ref[...], w_ref[0:CONV_CH, :], preferred_element_type=F32)
    acc = acc + jnp.dot(b_ref[...], w_ref[CONV_CH:, :], preferred_element_type=F32)
    o_ref[...] = x_ref[...] + acc


def _out_proj(x2, a2, b2, w_bf, tm=512):
    m, d = x2.shape
    return pl.pallas_call(
        _out_proj_kernel,
        grid=(m // tm,),
        in_specs=[
            pl.BlockSpec((tm, d), lambda i: (i, 0)),
            pl.BlockSpec((tm, CONV_CH), lambda i: (i, 0)),
            pl.BlockSpec((tm, ATTN_WIDTH), lambda i: (i, 0)),
            pl.BlockSpec((d, d), lambda i: (0, 0)),
        ],
        out_specs=pl.BlockSpec((tm, d), lambda i: (i, 0)),
        out_shape=jax.ShapeDtypeStruct((m, d), F32),
        compiler_params=_cparams(("parallel",)),
        name="out_proj",
    )(x2, a2, b2, w_bf)


FFN_TM = 512
FFN_TF = 512
FFN_PAD = BF16_ROWS
FFN_HALO = 8


def _ffn_kernel(x_ref, xp_ref, xn_ref, g_ref, wg_ref, wv_ref, cwg_ref, cwv_ref,
                cbg_ref, cbv_ref, wd_ref, o_ref, h_scr, *, tiles_per_seq):
    i = pl.program_id(0)
    j = pl.program_id(1)
    tm = FFN_TM

    def norm(t):
        ms = jnp.mean(t * t, axis=-1, keepdims=True)
        return t * lax.rsqrt(ms + RMS_EPS) * g_ref[...]

    @pl.when(j == 0)
    def _():
        keep_prev = ((i % tiles_per_seq) != 0).astype(F32)
        keep_next = ((i % tiles_per_seq) != tiles_per_seq - 1).astype(F32)
        rid = lax.broadcasted_iota(jnp.int32, (FFN_PAD, 1), 0)
        hp = norm(xp_ref[...])[FFN_HALO - 1:FFN_HALO, :] * keep_prev
        hp = jnp.where(rid == FFN_PAD - 1, hp, 0.0)
        hn = norm(xn_ref[...])[0:1, :] * keep_next
        hn = jnp.where(rid == 0, hn, 0.0)
        h_scr[0:FFN_PAD, :] = hp.astype(BF16)
        h_scr[FFN_PAD:FFN_PAD + tm, :] = norm(x_ref[...]).astype(BF16)
        h_scr[FFN_PAD + tm:, :] = hn.astype(BF16)
        o_ref[...] = x_ref[...]

    hh = h_scr[...]

    def up_conv(w_ref, cw_ref, cb_ref):
        u = jnp.dot(hh, w_ref[...], preferred_element_type=F32)
        return (cw_ref[0:1, :] * u[FFN_PAD - 1:FFN_PAD - 1 + tm]
                + cw_ref[1:2, :] * u[FFN_PAD:FFN_PAD + tm]
                + cw_ref[2:3, :] * u[FFN_PAD + 1:FFN_PAD + 1 + tm]
                + cb_ref[...])

    gte = up_conv(wg_ref, cwg_ref, cbg_ref)
    val = up_conv(wv_ref, cwv_ref, cbv_ref)
    act = (gte * jax.nn.sigmoid(gte) * val).astype(BF16)
    o_ref[...] += jnp.dot(act, wd_ref[...], preferred_element_type=F32)


def _conv_ffn(x1, norm2_g, w_up_bf, ffn_dw_w, ffn_dw_b, w_down_bf, seq_len):
    m, d = x1.shape
    tm, tf = FFN_TM, FFN_TF
    nf = FFN_DIM // tf
    hb = tm // FFN_HALO
    n_hblk = m // FFN_HALO
    kern = functools.partial(_ffn_kernel, tiles_per_seq=seq_len // tm)
    return pl.pallas_call(
        kern,
        grid=(m // tm, nf),
        in_specs=[
            pl.BlockSpec((tm, d), lambda i, j: (i, 0)),
            pl.BlockSpec((FFN_HALO, d), lambda i, j: (jnp.maximum(i * hb - 1, 0), 0)),
            pl.BlockSpec((FFN_HALO, d), lambda i, j: (jnp.minimum((i + 1) * hb, n_hblk - 1), 0)),
            pl.BlockSpec((1, d), lambda i, j: (0, 0)),
            pl.BlockSpec((d, tf), lambda i, j: (0, j)),
            pl.BlockSpec((d, tf), lambda i, j: (0, nf + j)),
            pl.BlockSpec((3, tf), lambda i, j: (0, j)),
            pl.BlockSpec((3, tf), lambda i, j: (0, nf + j)),
            pl.BlockSpec((1, tf), lambda i, j: (0, j)),
            pl.BlockSpec((1, tf), lambda i, j: (0, nf + j)),
            pl.BlockSpec((tf, d), lambda i, j: (j, 0)),
        ],
        out_specs=pl.BlockSpec((tm, d), lambda i, j: (i, 0)),
        out_shape=jax.ShapeDtypeStruct((m, d), F32),
        scratch_shapes=[pltpu.VMEM((tm + 2 * FFN_PAD, d), BF16)],
        compiler_params=_cparams(("parallel", "arbitrary")),
        name="conv_ffn",
    )(x1, x1, x1, norm2_g.reshape(1, d), w_up_bf, w_up_bf, ffn_dw_w, ffn_dw_w,
      ffn_dw_b.reshape(1, -1), ffn_dw_b.reshape(1, -1), w_down_bf)


def kernel(x, norm1_g, w_in, conv_dw_w, conv_dw_b, conv_ln_g, conv_ln_b, q_norm_g, k_norm_g,
           w_out, norm2_g, w_up, ffn_dw_w, ffn_dw_b, w_down):
    b, s, d = x.shape
    x2 = x.reshape(b * s, d)
    u = _in_proj(x2, norm1_g, w_in.astype(BF16))
    u3 = u.reshape(b, s, IN_COLS)
    a_out = _conv_group(u3, conv_dw_w, conv_dw_b, conv_ln_g, conv_ln_b)
    b_out = _attn_group(u3, q_norm_g, k_norm_g)
    x1 = _out_proj(x2, a_out.reshape(b * s, CONV_CH), b_out.reshape(b * s, ATTN_WIDTH),
                   w_out.astype(BF16))
    y = _conv_ffn(x1, norm2_g, w_up.astype(BF16), ffn_dw_w, ffn_dw_b, w_down.astype(BF16), s)
    return y.reshape(b, s, d)
```

```python
import functools

import jax
import jax.numpy as jnp
from jax import lax
from jax.experimental import pallas as pl
from jax.experimental.pallas import tpu as pltpu

D_MODEL = 2048
CONV_CH = 1024
ATTN_WIDTH = 1024
HEAD_DIM = 128
N_HEADS = ATTN_WIDTH // HEAD_DIM
CONV_WIDTH = 31
CONV_HALF = (CONV_WIDTH - 1) // 2
FFN_DIM = 5632
IN_COLS = 2 * CONV_CH + 3 * ATTN_WIDTH
RMS_EPS = 1e-6
LN_EPS = 1e-5
NEG_BIG = -1e30
BAND_R = 64
DILATIONS = (1, 4, 16)

LANES = 128
BF16_ROWS = 16
VMEM_LIMIT = 56 * 1024 * 1024

F32 = jnp.float32
BF16 = jnp.bfloat16


def _cparams(sem):
    return pltpu.CompilerParams(dimension_semantics=sem, vmem_limit_bytes=VMEM_LIMIT)


def _in_proj_kernel(x_ref, g_ref, w_ref, o_ref, h_scr):
    @pl.when(pl.program_id(1) == 0)
    def _():
        xf = x_ref[...]
        ms = jnp.mean(xf * xf, axis=-1, keepdims=True)
        h_scr[...] = (xf * lax.rsqrt(ms + RMS_EPS) * g_ref[...]).astype(BF16)

    o_ref[...] = jnp.dot(h_scr[...], w_ref[...], preferred_element_type=F32)


def _in_proj(x2, g, w_bf, tm=1024, tn=512):
    m, d = x2.shape
    n = w_bf.shape[1]
    return pl.pallas_call(
        _in_proj_kernel,
        grid=(m // tm, n // tn),
        in_specs=[
            pl.BlockSpec((tm, d), lambda i, j: (i, 0)),
            pl.BlockSpec((1, d), lambda i, j: (0, 0)),
            pl.BlockSpec((d, tn), lambda i, j: (0, j)),
        ],
        out_specs=pl.BlockSpec((tm, tn), lambda i, j: (i, j)),
        out_shape=jax.ShapeDtypeStruct((m, n), F32),
        scratch_shapes=[pltpu.VMEM((tm, d), BF16)],
        compiler_params=_cparams(("parallel", "arbitrary")),
        name="in_proj",
    )(x2, g.reshape(1, d), w_bf)


CONV_TT = 256
CONV_HALO = 16
CONV_RC = 64
CONV_LB = CONV_CH // LANES


def _conv_kernel(val_ref, gate_ref, pval_ref, pgate_ref, nval_ref, ngate_ref,
                 w_ref, b_ref, lg_ref, lb_ref, o_ref, a_scr, y_scr):
    ti = pl.program_id(1)
    nt = pl.num_programs(1)
    tt = CONV_TT

    def glu(v, g):
        return v * jax.nn.sigmoid(g)

    keep_prev = (ti > 0).astype(F32)
    keep_next = (ti < nt - 1).astype(F32)
    for lb in range(CONV_LB):
        ls = slice(lb * LANES, (lb + 1) * LANES)
        a_scr[lb, 0:CONV_HALO, :] = glu(pval_ref[0, :, ls], pgate_ref[0, :, ls]) * keep_prev
        a_scr[lb, CONV_HALO:CONV_HALO + tt, :] = glu(val_ref[0, :, ls], gate_ref[0, :, ls])
        a_scr[lb, CONV_HALO + tt:, :] = glu(nval_ref[0, :, ls], ngate_ref[0, :, ls]) * keep_next

    def lane_block(lb, carry):
        for rc in range(tt // CONV_RC):
            r0 = rc * CONV_RC + CONV_HALO - CONV_HALF
            acc = jnp.broadcast_to(b_ref[lb], (CONV_RC, LANES))
            for k in range(CONV_WIDTH):
                acc = acc + a_scr[lb, r0 + k:r0 + k + CONV_RC, :] * w_ref[lb, k:k + 1, :]
            y_scr[lb, rc * CONV_RC:(rc + 1) * CONV_RC, :] = acc
        return carry

    lax.fori_loop(0, CONV_LB, lane_block, 0)

    tot = y_scr[0]
    for lb in range(1, CONV_LB):
        tot = tot + y_scr[lb]
    mu = jnp.sum(tot, axis=-1, keepdims=True) * (1.0 / CONV_CH)
    sq = None
    for lb in range(CONV_LB):
        c = y_scr[lb] - mu
        sq = c * c if sq is None else sq + c * c
    var = jnp.sum(sq, axis=-1, keepdims=True) * (1.0 / CONV_CH)
    rstd = lax.rsqrt(var + LN_EPS)
    for lb in range(CONV_LB):
        ls = slice(lb * LANES, (lb + 1) * LANES)
        z = (y_scr[lb] - mu) * rstd * lg_ref[:, ls] + lb_ref[:, ls]
        o_ref[0, :, ls] = (z * jax.nn.sigmoid(z)).astype(o_ref.dtype)


def _conv_group(u3, conv_dw_w, conv_dw_b, conv_ln_g, conv_ln_b):
    b, s, _ = u3.shape
    tt, halo = CONV_TT, CONV_HALO
    nt = s // tt
    hb = tt // halo
    n_hblk = s // halo
    w3 = conv_dw_w.reshape(CONV_WIDTH, CONV_LB, LANES).transpose(1, 0, 2)
    b3 = conv_dw_b.reshape(CONV_LB, 1, LANES)
    main = lambda col: pl.BlockSpec((1, tt, CONV_CH), lambda bi, ti: (bi, ti, col))
    prev = lambda col: pl.BlockSpec(
        (1, halo, CONV_CH), lambda bi, ti: (bi, jnp.maximum(ti * hb - 1, 0), col))
    nxt = lambda col: pl.BlockSpec(
        (1, halo, CONV_CH), lambda bi, ti: (bi, jnp.minimum((ti + 1) * hb, n_hblk - 1), col))
    full = lambda shape: pl.BlockSpec(shape, lambda bi, ti: (0,) * len(shape))
    return pl.pallas_call(
        _conv_kernel,
        grid=(b, nt),
        in_specs=[main(0), main(1), prev(0), prev(1), nxt(0), nxt(1),
                  full((CONV_LB, CONV_WIDTH, LANES)), full((CONV_LB, 1, LANES)),
                  full((1, CONV_CH)), full((1, CONV_CH))],
        out_specs=pl.BlockSpec((1, tt, CONV_CH), lambda bi, ti: (bi, ti, 0)),
        out_shape=jax.ShapeDtypeStruct((b, s, CONV_CH), BF16),
        scratch_shapes=[pltpu.VMEM((CONV_LB, tt + 2 * halo, LANES), F32),
                        pltpu.VMEM((CONV_LB, tt, LANES), F32)],
        compiler_params=_cparams(("parallel", "arbitrary")),
        name="conv_group",
    )(u3, u3, u3, u3, u3, u3, w3, b3, conv_ln_g.reshape(1, CONV_CH), conv_ln_b.reshape(1, CONV_CH))


ATT_QB = 128
ATT_KB = ATT_QB + 2 * BAND_R


def _attn_kernel(slope_ref, q_ref, k_ref, v_ref, qg_ref, kg_ref, o_ref,
                 qn, kn, bias_scr, acc_scr, m_scr, l_scr):
    h = pl.program_id(1)
    s_len = q_ref.shape[1]
    slope = slope_ref[h]

    def headnorm(t, g, scale):
        ms = jnp.mean(t * t, axis=-1, keepdims=True)
        return t * lax.rsqrt(ms + RMS_EPS) * g * scale

    qn[...] = headnorm(q_ref[0], qg_ref[...], HEAD_DIM ** -0.5)
    kn[...] = headnorm(k_ref[0], kg_ref[...], 1.0)

    rows = lax.broadcasted_iota(jnp.int32, (ATT_QB, ATT_KB), 0)
    cols = lax.broadcasted_iota(jnp.int32, (ATT_QB, ATT_KB), 1)
    for w, dil in enumerate(DILATIONS):
        for e in range(3):
            off = jnp.abs(cols - rows - e * BAND_R)
            bias = jnp.where(off <= BAND_R, -(slope * dil) * off.astype(F32), NEG_BIG)
            bias_scr[3 * w + e] = bias

    ones_rhs = jnp.ones((ATT_KB, LANES), BF16)

    def block(w, q_idx, k_idx, nk, bias):
        qb = qn[q_idx, :].astype(BF16)
        kb = kn[k_idx, :].astype(BF16)
        vb = v_ref[0, k_idx, :].astype(BF16)
        sc = lax.dot_general(qb, kb, (((1,), (1,)), ((), ())), preferred_element_type=F32)
        sc = sc + bias
        m = jnp.max(sc, axis=-1, keepdims=True)
        p = jnp.exp(sc - m).astype(BF16)
        acc = jnp.dot(p, jnp.concatenate([vb, ones_rhs[:nk]], axis=1),
                      preferred_element_type=F32)
        acc_scr[w, q_idx, :] = acc[:, :HEAD_DIM]
        l_scr[w, q_idx, :] = acc[:, HEAD_DIM:]
        m_scr[w, q_idx, :] = jnp.broadcast_to(m, (ATT_QB, LANES))

    for w, dil in enumerate(DILATIONS):
        class_len = s_len // dil
        nk = min(ATT_KB, class_len)
        for c in range(dil):
            for i in range(class_len // ATT_QB):
                q0 = i * ATT_QB
                k0 = min(max(q0 - BAND_R, 0), class_len - nk)
                if dil == 1:
                    q_idx = pl.ds(q0, ATT_QB)
                    k_idx = pl.ds(k0, nk)
                else:
                    q_idx = pl.ds(c + dil * q0, ATT_QB, stride=dil)
                    k_idx = pl.ds(c + dil * k0, nk, stride=dil)
                bias = bias_scr[3 * w + (q0 - k0) // BAND_R]
                if nk < ATT_KB:
                    bias = bias[:, :nk]
                block(w, q_idx, k_idx, nk, bias)

    chunk = 256
    for r in range(s_len // chunk):
        rs = slice(r * chunk, (r + 1) * chunk)
        m0, m1, m2 = m_scr[0, rs, :], m_scr[1, rs, :], m_scr[2, rs, :]
        mm = jnp.maximum(jnp.maximum(m0, m1), m2)
        a0, a1, a2 = jnp.exp(m0 - mm), jnp.exp(m1 - mm), jnp.exp(m2 - mm)
        num = a0 * acc_scr[0, rs, :] + a1 * acc_scr[1, rs, :] + a2 * acc_scr[2, rs, :]
        den = a0 * l_scr[0, rs, :] + a1 * l_scr[1, rs, :] + a2 * l_scr[2, rs, :]
        o_ref[0, rs, :] = (num / den).astype(o_ref.dtype)


def _attn_group(u3, q_norm_g, k_norm_g):
    b, s, _ = u3.shape
    qcol = 2 * CONV_CH // HEAD_DIM
    slopes = jnp.asarray([2.0 ** (-8.0 * (i + 1) / N_HEADS) for i in range(N_HEADS)], F32)
    head = lambda base: pl.BlockSpec((1, s, HEAD_DIM), lambda bi, hi: (bi, 0, base + hi))
    gain = pl.BlockSpec((1, HEAD_DIM), lambda bi, hi: (0, 0))
    return pl.pallas_call(
        _attn_kernel,
        grid=(b, N_HEADS),
        in_specs=[pl.BlockSpec(memory_space=pltpu.SMEM),
                  head(qcol), head(qcol + N_HEADS), head(qcol + 2 * N_HEADS), gain, gain],
        out_specs=pl.BlockSpec((1, s, HEAD_DIM), lambda bi, hi: (bi, 0, hi)),
        out_shape=jax.ShapeDtypeStruct((b, s, ATTN_WIDTH), BF16),
        scratch_shapes=[pltpu.VMEM((s, HEAD_DIM), F32),
                        pltpu.VMEM((s, HEAD_DIM), F32),
                        pltpu.VMEM((9, ATT_QB, ATT_KB), F32),
                        pltpu.VMEM((3, s, HEAD_DIM), F32),
                        pltpu.VMEM((3, s, LANES), F32),
                        pltpu.VMEM((3, s, LANES), F32)],
        compiler_params=_cparams(("parallel", "arbitrary")),
        name="attn_group",
    )(slopes, u3, u3, u3, q_norm_g.reshape(1, HEAD_DIM), k_norm_g.reshape(1, HEAD_DIM))


def _out_proj_kernel(x_ref, a_ref, b_ref, w_ref, o_ref):
    acc = jnp.dot(a_ref[...], w_ref[0:CONV_CH, :], preferred_element_type=F32)
    acc = acc + jnp.dot(b_ref[...], w_ref[CONV_CH:, :], preferred_element_type=F32)
    o_ref[...] = x_ref[...] + acc


def _out_proj(x2, a2, b2, w_bf, tm=512):
    m, d = x2.shape
    return pl.pallas_call(
        _out_proj_kernel,
        grid=(m // tm,),
        in_specs=[
            pl.BlockSpec((tm, d), lambda i: (i, 0)),
            pl.BlockSpec((tm, CONV_CH), lambda i: (i, 0)),
            pl.BlockSpec((tm, ATTN_WIDTH), lambda i: (i, 0)),
            pl.BlockSpec((d, d), lambda i: (0, 0)),
        ],
        out_specs=pl.BlockSpec((tm, d), lambda i: (i, 0)),
        out_shape=jax.ShapeDtypeStruct((m, d), F32),
        compiler_params=_cparams(("parallel",)),
        name="out_proj",
    )(x2, a2, b2, w_bf)


FFN_TM = 512
FFN_TF = 512
FFN_PAD = BF16_ROWS
FFN_HALO = 8


def _ffn_kernel(x_ref, xp_ref, xn_ref, g_ref, wg_ref, wv_ref, cwg_ref, cwv_ref,
                cbg_ref, cbv_ref, wd_ref, o_ref, h_scr, ug_scr, uv_scr, *, tiles_per_seq):
    i = pl.program_id(0)
    j = pl.program_id(1)
    tm = FFN_TM

    def norm(t):
        ms = jnp.mean(t * t, axis=-1, keepdims=True)
        return t * lax.rsqrt(ms + RMS_EPS) * g_ref[...]

    @pl.when(j == 0)
    def _():
        keep_prev = ((i % tiles_per_seq) != 0).astype(F32)
        keep_next = ((i % tiles_per_seq) != tiles_per_seq - 1).astype(F32)
        rid = lax.broadcasted_iota(jnp.int32, (FFN_PAD, 1), 0)
        hp = norm(xp_ref[...])[FFN_HALO - 1:FFN_HALO, :] * keep_prev
        hp = jnp.where(rid == FFN_PAD - 1, hp, 0.0)
        hn = norm(xn_ref[...])[0:1, :] * keep_next
        hn = jnp.where(rid == 0, hn, 0.0)
        h_scr[0:FFN_PAD, :] = hp.astype(BF16)
        h_scr[FFN_PAD:FFN_PAD + tm, :] = norm(x_ref[...]).astype(BF16)
        h_scr[FFN_PAD + tm:, :] = hn.astype(BF16)
        o_ref[...] = x_ref[...]

    hh = h_scr[...]

    def up_conv(w_ref, cw_ref, cb_ref, u_scr):
        u = jnp.dot(hh, w_ref[...], preferred_element_type=F32)
        cols = []
        for c in range(FFN_TF // LANES):
            ls = slice(c * LANES, (c + 1) * LANES)
            u_scr[c] = u[:, ls]
            cols.append(cw_ref[0:1, ls] * u_scr[c, FFN_PAD - 1:FFN_PAD - 1 + tm, :]
                        + cw_ref[1:2, ls] * u_scr[c, FFN_PAD:FFN_PAD + tm, :]
                        + cw_ref[2:3, ls] * u_scr[c, FFN_PAD + 1:FFN_PAD + 1 + tm, :]
                        + cb_ref[:, ls])
        return jnp.concatenate(cols, axis=1)

    gte = up_conv(wg_ref, cwg_ref, cbg_ref, ug_scr)
    val = up_conv(wv_ref, cwv_ref, cbv_ref, uv_scr)
    act = (gte * jax.nn.sigmoid(gte) * val).astype(BF16)
    o_ref[...] += jnp.dot(act, wd_ref[...], preferred_element_type=F32)


def _conv_ffn(x1, norm2_g, w_up_bf, ffn_dw_w, ffn_dw_b, w_down_bf, seq_len):
    m, d = x1.shape
    tm, tf = FFN_TM, FFN_TF
    nf = FFN_DIM // tf
    hb = tm // FFN_HALO
    n_hblk = m // FFN_HALO
    kern = functools.partial(_ffn_kernel, tiles_per_seq=seq_len // tm)
    return pl.pallas_call(
        kern,
        grid=(m // tm, nf),
        in_specs=[
            pl.BlockSpec((tm, d), lambda i, j: (i, 0)),
            pl.BlockSpec((FFN_HALO, d), lambda i, j: (jnp.maximum(i * hb - 1, 0), 0)),
            pl.BlockSpec((FFN_HALO, d), lambda i, j: (jnp.minimum((i + 1) * hb, n_hblk - 1), 0)),
            pl.BlockSpec((1, d), lambda i, j: (0, 0)),
            pl.BlockSpec((d, tf), lambda i, j: (0, j)),
            pl.BlockSpec((d, tf), lambda i, j: (0, nf + j)),
            pl.BlockSpec((3, tf), lambda i, j: (0, j)),
            pl.BlockSpec((3, tf), lambda i, j: (0, nf + j)),
            pl.BlockSpec((1, tf), lambda i, j: (0, j)),
            pl.BlockSpec((1, tf), lambda i, j: (0, nf + j)),
            pl.BlockSpec((tf, d), lambda i, j: (j, 0)),
        ],
        out_specs=pl.BlockSpec((tm, d), lambda i, j: (i, 0)),
        out_shape=jax.ShapeDtypeStruct((m, d), F32),
        scratch_shapes=[pltpu.VMEM((tm + 2 * FFN_PAD, d), BF16),
                        pltpu.VMEM((tf // LANES, tm + 2 * FFN_PAD, LANES), F32),
                        pltpu.VMEM((tf // LANES, tm + 2 * FFN_PAD, LANES), F32)],
        compiler_params=_cparams(("parallel", "arbitrary")),
        name="conv_ffn",
    )(x1, x1, x1, norm2_g.reshape(1, d), w_up_bf, w_up_bf, ffn_dw_w, ffn_dw_w,
      ffn_dw_b.reshape(1, -1), ffn_dw_b.reshape(1, -1), w_down_bf)


def kernel(x, norm1_g, w_in, conv_dw_w, conv_dw_b, conv_ln_g, conv_ln_b, q_norm_g, k_norm_g,
           w_out, norm2_g, w_up, ffn_dw_w, ffn_dw_b, w_down):
    b, s, d = x.shape
    x2 = x.reshape(b * s, d)
    u = _in_proj(x2, norm1_g, w_in.astype(BF16))
    u3 = u.reshape(b, s, IN_COLS)
    a_out = _conv_group(u3, conv_dw_w, conv_dw_b, conv_ln_g, conv_ln_b)
    b_out = _attn_group(u3, q_norm_g, k_norm_g)
    x1 = _out_proj(x2, a_out.reshape(b * s, CONV_CH), b_out.reshape(b * s, ATTN_WIDTH),
                   w_out.astype(BF16))
    y = _conv_ffn(x1, norm2_g, w_up.astype(BF16), ffn_dw_w, ffn_dw_b, w_down.astype(BF16), s)
    return y.reshape(b, s, d)
```

```python
import functools

import jax
import jax.numpy as jnp
from jax import lax
from jax.experimental import pallas as pl
from jax.experimental.pallas import tpu as pltpu

D_MODEL = 2048
CONV_CH = 1024
ATTN_WIDTH = 1024
HEAD_DIM = 128
N_HEADS = ATTN_WIDTH // HEAD_DIM
CONV_WIDTH = 31
CONV_HALF = (CONV_WIDTH - 1) // 2
FFN_DIM = 5632
IN_COLS = 2 * CONV_CH + 3 * ATTN_WIDTH
RMS_EPS = 1e-6
LN_EPS = 1e-5
NEG_BIG = -1e30
BAND_R = 64
DILATIONS = (1, 4, 16)

LANES = 128
BF16_ROWS = 16
VMEM_LIMIT = 56 * 1024 * 1024

F32 = jnp.float32
BF16 = jnp.bfloat16


def _cparams(sem):
    return pltpu.CompilerParams(dimension_semantics=sem, vmem_limit_bytes=VMEM_LIMIT)


IN_TM = 1024
IN_TN = 512
Q_BLK0 = 2 * CONV_CH // IN_TN
K_BLK0 = Q_BLK0 + ATTN_WIDTH // IN_TN
V_BLK0 = K_BLK0 + ATTN_WIDTH // IN_TN
LOG2E = 1.4426950408889634


def _in_proj_kernel(x_ref, g_ref, w_ref, qg_ref, kg_ref, o_ref, h_scr):
    j = pl.program_id(1)

    @pl.when(j == 0)
    def _():
        xf = x_ref[...]
        ms = jnp.mean(xf * xf, axis=-1, keepdims=True)
        h_scr[...] = (xf * lax.rsqrt(ms + RMS_EPS) * g_ref[...]).astype(BF16)

    def proj():
        return jnp.dot(h_scr[...], w_ref[...], preferred_element_type=F32)

    def store_headnorm(gain):
        acc = proj()
        for hd in range(IN_TN // HEAD_DIM):
            ls = slice(hd * HEAD_DIM, (hd + 1) * HEAD_DIM)
            t = acc[:, ls]
            ms = jnp.mean(t * t, axis=-1, keepdims=True)
            o_ref[:, ls] = t * lax.rsqrt(ms + RMS_EPS) * gain

    @pl.when((j < Q_BLK0) | (j >= V_BLK0))
    def _():
        o_ref[...] = proj()

    @pl.when((j >= Q_BLK0) & (j < K_BLK0))
    def _():
        store_headnorm(qg_ref[...] * (HEAD_DIM ** -0.5 * LOG2E))

    @pl.when((j >= K_BLK0) & (j < V_BLK0))
    def _():
        store_headnorm(kg_ref[...])


def _in_proj(x2, g, w_bf, q_norm_g, k_norm_g):
    m, d = x2.shape
    n = w_bf.shape[1]
    tm, tn = IN_TM, IN_TN
    gain = pl.BlockSpec((1, HEAD_DIM), lambda i, j: (0, 0))
    return pl.pallas_call(
        _in_proj_kernel,
        grid=(m // tm, n // tn),
        in_specs=[
            pl.BlockSpec((tm, d), lambda i, j: (i, 0)),
            pl.BlockSpec((1, d), lambda i, j: (0, 0)),
            pl.BlockSpec((d, tn), lambda i, j: (0, j)),
            gain, gain,
        ],
        out_specs=pl.BlockSpec((tm, tn), lambda i, j: (i, j)),
        out_shape=jax.ShapeDtypeStruct((m, n), F32),
        scratch_shapes=[pltpu.VMEM((tm, d), BF16)],
        compiler_params=_cparams(("parallel", "arbitrary")),
        name="in_proj",
    )(x2, g.reshape(1, d), w_bf, q_norm_g.reshape(1, HEAD_DIM), k_norm_g.reshape(1, HEAD_DIM))


CONV_TT = 256
CONV_HALO = 16
CONV_RC = 64
CONV_LB = CONV_CH // LANES


def _conv_kernel(val_ref, gate_ref, pval_ref, pgate_ref, nval_ref, ngate_ref,
                 w_ref, b_ref, lg_ref, lb_ref, o_ref, a_scr, y_scr):
    ti = pl.program_id(1)
    nt = pl.num_programs(1)
    tt = CONV_TT

    def glu(v, g):
        return v * jax.nn.sigmoid(g)

    keep_prev = (ti > 0).astype(F32)
    keep_next = (ti < nt - 1).astype(F32)
    for lb in range(CONV_LB):
        ls = slice(lb * LANES, (lb + 1) * LANES)
        a_scr[lb, 0:CONV_HALO, :] = glu(pval_ref[0, :, ls], pgate_ref[0, :, ls]) * keep_prev
        a_scr[lb, CONV_HALO:CONV_HALO + tt, :] = glu(val_ref[0, :, ls], gate_ref[0, :, ls])
        a_scr[lb, CONV_HALO + tt:, :] = glu(nval_ref[0, :, ls], ngate_ref[0, :, ls]) * keep_next

    def lane_block(lb, carry):
        for rc in range(tt // CONV_RC):
            r0 = rc * CONV_RC + CONV_HALO - CONV_HALF
            acc = jnp.broadcast_to(b_ref[lb], (CONV_RC, LANES))
            for k in range(CONV_WIDTH):
                acc = acc + a_scr[lb, r0 + k:r0 + k + CONV_RC, :] * w_ref[lb, k:k + 1, :]
            y_scr[lb, rc * CONV_RC:(rc + 1) * CONV_RC, :] = acc
        return carry

    lax.fori_loop(0, CONV_LB, lane_block, 0)

    tot = y_scr[0]
    for lb in range(1, CONV_LB):
        tot = tot + y_scr[lb]
    mu = jnp.sum(tot, axis=-1, keepdims=True) * (1.0 / CONV_CH)
    sq = None
    for lb in range(CONV_LB):
        c = y_scr[lb] - mu
        sq = c * c if sq is None else sq + c * c
    var = jnp.sum(sq, axis=-1, keepdims=True) * (1.0 / CONV_CH)
    rstd = lax.rsqrt(var + LN_EPS)
    for lb in range(CONV_LB):
        ls = slice(lb * LANES, (lb + 1) * LANES)
        z = (y_scr[lb] - mu) * rstd * lg_ref[:, ls] + lb_ref[:, ls]
        o_ref[0, :, ls] = (z * jax.nn.sigmoid(z)).astype(o_ref.dtype)


def _conv_group(u3, conv_dw_w, conv_dw_b, conv_ln_g, conv_ln_b):
    b, s, _ = u3.shape
    tt, halo = CONV_TT, CONV_HALO
    nt = s // tt
    hb = tt // halo
    n_hblk = s // halo
    w3 = conv_dw_w.reshape(CONV_WIDTH, CONV_LB, LANES).transpose(1, 0, 2)
    b3 = conv_dw_b.reshape(CONV_LB, 1, LANES)
    main = lambda col: pl.BlockSpec((1, tt, CONV_CH), lambda bi, ti: (bi, ti, col))
    prev = lambda col: pl.BlockSpec(
        (1, halo, CONV_CH), lambda bi, ti: (bi, jnp.maximum(ti * hb - 1, 0), col))
    nxt = lambda col: pl.BlockSpec(
        (1, halo, CONV_CH), lambda bi, ti: (bi, jnp.minimum((ti + 1) * hb, n_hblk - 1), col))
    full = lambda shape: pl.BlockSpec(shape, lambda bi, ti: (0,) * len(shape))
    return pl.pallas_call(
        _conv_kernel,
        grid=(b, nt),
        in_specs=[main(0), main(1), prev(0), prev(1), nxt(0), nxt(1),
                  full((CONV_LB, CONV_WIDTH, LANES)), full((CONV_LB, 1, LANES)),
                  full((1, CONV_CH)), full((1, CONV_CH))],
        out_specs=pl.BlockSpec((1, tt, CONV_CH), lambda bi, ti: (bi, ti, 0)),
        out_shape=jax.ShapeDtypeStruct((b, s, CONV_CH), BF16),
        scratch_shapes=[pltpu.VMEM((CONV_LB, tt + 2 * halo, LANES), F32),
                        pltpu.VMEM((CONV_LB, tt, LANES), F32)],
        compiler_params=_cparams(("parallel", "arbitrary")),
        name="conv_group",
    )(u3, u3, u3, u3, u3, u3, w3, b3, conv_ln_g.reshape(1, CONV_CH), conv_ln_b.reshape(1, CONV_CH))


ATT_QB = 128
ATT_KB = ATT_QB + 2 * BAND_R


ATT_C4 = 4


def _attn_kernel(slope_ref, q_ref, k_ref, v_ref, o_ref,
                 q4, k4, v4, bias_scr, acc_scr, m_scr, l_scr, out_scr):
    h = pl.program_id(1)
    s_len = q_ref.shape[1]
    cl = s_len // ATT_C4
    slope = slope_ref[h] * LOG2E

    for c4 in range(ATT_C4):
        dst = pl.ds(c4 * cl, cl)
        src = pl.ds(c4, cl, stride=ATT_C4)
        q4[dst, :] = q_ref[0, src, :]
        k4[dst, :] = k_ref[0, src, :]
        v4[dst, :] = v_ref[0, src, :]

    rows = lax.broadcasted_iota(jnp.int32, (ATT_QB, ATT_KB), 0)
    cols = lax.broadcasted_iota(jnp.int32, (ATT_QB, ATT_KB), 1)
    for w, dil in enumerate(DILATIONS):
        for e in range(3):
            off = jnp.abs(cols - rows - e * BAND_R)
            bias = jnp.where(off <= BAND_R, -(slope * dil) * off.astype(F32), NEG_BIG)
            bias_scr[3 * w + e] = bias

    ones_rhs = jnp.ones((ATT_KB, LANES), BF16)

    def block(w, qb, kb, vb, out_idx, bias):
        qb, kb, vb = qb.astype(BF16), kb.astype(BF16), vb.astype(BF16)
        sc = lax.dot_general(qb, kb, (((1,), (1,)), ((), ())), preferred_element_type=F32)
        sc = sc + bias
        m = jnp.max(sc, axis=-1, keepdims=True)
        p = jnp.exp2(sc - m).astype(BF16)
        acc = jnp.dot(p, jnp.concatenate([vb, ones_rhs[:vb.shape[0]]], axis=1),
                      preferred_element_type=F32)
        acc_scr[w, out_idx, :] = acc[:, :HEAD_DIM]
        l_scr[w, out_idx, :] = acc[:, HEAD_DIM:]
        m_scr[w, out_idx, :] = jnp.broadcast_to(m, (ATT_QB, LANES))

    def key_start(q0, class_len, nk):
        return min(max(q0 - BAND_R, 0), class_len - nk)

    for i in range(s_len // ATT_QB):
        q0 = i * ATT_QB
        k0 = key_start(q0, s_len, ATT_KB)
        qi, ki = pl.ds(q0, ATT_QB), pl.ds(k0, ATT_KB)
        block(0, q_ref[0, qi, :], k_ref[0, ki, :], v_ref[0, ki, :], qi,
              bias_scr[(q0 - k0) // BAND_R])

    for c4 in range(ATT_C4):
        for i in range(cl // ATT_QB):
            q0 = i * ATT_QB
            k0 = key_start(q0, cl, ATT_KB)
            qi, ki = pl.ds(c4 * cl + q0, ATT_QB), pl.ds(c4 * cl + k0, ATT_KB)
            block(1, q4[qi, :], k4[ki, :], v4[ki, :], qi, bias_scr[3 + (q0 - k0) // BAND_R])

    sub_len = s_len // DILATIONS[2]
    for c4 in range(ATT_C4):
        for c in range(DILATIONS[2] // ATT_C4):
            idx = pl.ds(c4 * cl + c, sub_len, stride=DILATIONS[2] // ATT_C4)
            block(2, q4[idx, :], k4[idx, :], v4[idx, :], idx, bias_scr[6][:, :sub_len])

    chunk = 256
    for c4 in range(ATT_C4):
        for r in range(cl // chunk):
            nat = pl.ds(c4 + ATT_C4 * r * chunk, chunk, stride=ATT_C4)
            grp = pl.ds(c4 * cl + r * chunk, chunk)
            m0, m1, m2 = m_scr[0, nat, :], m_scr[1, grp, :], m_scr[2, grp, :]
            mm = jnp.maximum(jnp.maximum(m0, m1), m2)
            a0, a1, a2 = jnp.exp2(m0 - mm), jnp.exp2(m1 - mm), jnp.exp2(m2 - mm)
            num = a0 * acc_scr[0, nat, :] + a1 * acc_scr[1, grp, :] + a2 * acc_scr[2, grp, :]
            den = a0 * l_scr[0, nat, :] + a1 * l_scr[1, grp, :] + a2 * l_scr[2, grp, :]
            out_scr[nat, :] = num / den
    o_ref[0] = out_scr[...].astype(o_ref.dtype)


def _attn_group(u3):
    b, s, _ = u3.shape
    qcol = 2 * CONV_CH // HEAD_DIM
    slopes = jnp.asarray([2.0 ** (-8.0 * (i + 1) / N_HEADS) for i in range(N_HEADS)], F32)
    head = lambda base: pl.BlockSpec((1, s, HEAD_DIM), lambda bi, hi: (bi, 0, base + hi))
    return pl.pallas_call(
        _attn_kernel,
        grid=(b, N_HEADS),
        in_specs=[pl.BlockSpec(memory_space=pltpu.SMEM),
                  head(qcol), head(qcol + N_HEADS), head(qcol + 2 * N_HEADS)],
        out_specs=pl.BlockSpec((1, s, HEAD_DIM), lambda bi, hi: (bi, 0, hi)),
        out_shape=jax.ShapeDtypeStruct((b, s, ATTN_WIDTH), BF16),
        scratch_shapes=[pltpu.VMEM((s, HEAD_DIM), F32),
                        pltpu.VMEM((s, HEAD_DIM), F32),
                        pltpu.VMEM((s, HEAD_DIM), F32),
                        pltpu.VMEM((9, ATT_QB, ATT_KB), F32),
                        pltpu.VMEM((3, s, HEAD_DIM), F32),
                        pltpu.VMEM((3, s, LANES), F32),
                        pltpu.VMEM((3, s, LANES), F32),
                        pltpu.VMEM((s, HEAD_DIM), F32)],
        compiler_params=_cparams(("parallel", "arbitrary")),
        name="attn_group",
    )(slopes, u3, u3, u3)


def _out_proj_kernel(x_ref, a_ref, b_ref, w_ref, o_ref):
    acc = jnp.dot(a_ref[...], w_ref[0:CONV_CH, :], preferred_element_type=F32)
    acc = acc + jnp.dot(b_ref[...], w_ref[CONV_CH:, :], preferred_element_type=F32)
    o_ref[...] = x_ref[...] + acc


def _out_proj(x2, a2, b2, w_bf, tm=512):
    m, d = x2.shape
    return pl.pallas_call(
        _out_proj_kernel,
        grid=(m // tm,),
        in_specs=[
            pl.BlockSpec((tm, d), lambda i: (i, 0)),
            pl.BlockSpec((tm, CONV_CH), lambda i: (i, 0)),
            pl.BlockSpec((tm, ATTN_WIDTH), lambda i: (i, 0)),
            pl.BlockSpec((d, d), lambda i: (0, 0)),
        ],
        out_specs=pl.BlockSpec((tm, d), lambda i: (i, 0)),
        out_shape=jax.ShapeDtypeStruct((m, d), F32),
        compiler_params=_cparams(("parallel",)),
        name="out_proj",
    )(x2, a2, b2, w_bf)


FFN_TM = 512
FFN_TF = 512
FFN_PAD = BF16_ROWS
FFN_HALO = 8


def _ffn_kernel(x_ref, xp_ref, xn_ref, g_ref, wg_ref, wv_ref, cwg_ref, cwv_ref,
                cbg_ref, cbv_ref, wd_ref, o_ref, h_scr, ug_scr, uv_scr, *, tiles_per_seq):
    i = pl.program_id(0)
    j = pl.program_id(1)
    tm = FFN_TM

    def norm(t):
        ms = jnp.mean(t * t, axis=-1, keepdims=True)
        return t * lax.rsqrt(ms + RMS_EPS) * g_ref[...]

    @pl.when(j == 0)
    def _():
        keep_prev = ((i % tiles_per_seq) != 0).astype(F32)
        keep_next = ((i % tiles_per_seq) != tiles_per_seq - 1).astype(F32)
        rid = lax.broadcasted_iota(jnp.int32, (FFN_PAD, 1), 0)
        hp = norm(xp_ref[...])[FFN_HALO - 1:FFN_HALO, :] * keep_prev
        hp = jnp.where(rid == FFN_PAD - 1, hp, 0.0)
        hn = norm(xn_ref[...])[0:1, :] * keep_next
        hn = jnp.where(rid == 0, hn, 0.0)
        h_scr[0:FFN_PAD, :] = hp.astype(BF16)
        h_scr[FFN_PAD:FFN_PAD + tm, :] = norm(x_ref[...]).astype(BF16)
        h_scr[FFN_PAD + tm:, :] = hn.astype(BF16)
        o_ref[...] = x_ref[...]

    hh = h_scr[...]

    def up_conv(w_ref, cw_ref, cb_ref, u_scr):
        u = jnp.dot(hh, w_ref[...], preferred_element_type=F32)
        cols = []
        for c in range(FFN_TF // LANES):
            ls = slice(c * LANES, (c + 1) * LANES)
            u_scr[c] = u[:, ls]
            cols.append(cw_ref[0:1, ls] * u_scr[c, FFN_PAD - 1:FFN_PAD - 1 + tm, :]
                        + cw_ref[1:2, ls] * u_scr[c, FFN_PAD:FFN_PAD + tm, :]
                        + cw_ref[2:3, ls] * u_scr[c, FFN_PAD + 1:FFN_PAD + 1 + tm, :]
                        + cb_ref[:, ls])
        return jnp.concatenate(cols, axis=1)

    gte = up_conv(wg_ref, cwg_ref, cbg_ref, ug_scr)
    val = up_conv(wv_ref, cwv_ref, cbv_ref, uv_scr)
    act = (gte * jax.nn.sigmoid(gte) * val).astype(BF16)
    o_ref[...] += jnp.dot(act, wd_ref[...], preferred_element_type=F32)


def _conv_ffn(x1, norm2_g, w_up_bf, ffn_dw_w, ffn_dw_b, w_down_bf, seq_len):
    m, d = x1.shape
    tm, tf = FFN_TM, FFN_TF
    nf = FFN_DIM // tf
    hb = tm // FFN_HALO
    n_hblk = m // FFN_HALO
    kern = functools.partial(_ffn_kernel, tiles_per_seq=seq_len // tm)
    return pl.pallas_call(
        kern,
        grid=(m // tm, nf),
        in_specs=[
            pl.BlockSpec((tm, d), lambda i, j: (i, 0)),
            pl.BlockSpec((FFN_HALO, d), lambda i, j: (jnp.maximum(i * hb - 1, 0), 0)),
            pl.BlockSpec((FFN_HALO, d), lambda i, j: (jnp.minimum((i + 1) * hb, n_hblk - 1), 0)),
            pl.BlockSpec((1, d), lambda i, j: (0, 0)),
            pl.BlockSpec((d, tf), lambda i, j: (0, j)),
            pl.BlockSpec((d, tf), lambda i, j: (0, nf + j)),
            pl.BlockSpec((3, tf), lambda i, j: (0, j)),
            pl.BlockSpec((3, tf), lambda i, j: (0, nf + j)),
            pl.BlockSpec((1, tf), lambda i, j: (0, j)),
            pl.BlockSpec((1, tf), lambda i, j: (0, nf + j)),
            pl.BlockSpec((tf, d), lambda i, j: (j, 0)),
        ],
        out_specs=pl.BlockSpec((tm, d), lambda i, j: (i, 0)),
        out_shape=jax.ShapeDtypeStruct((m, d), F32),
        scratch_shapes=[pltpu.VMEM((tm + 2 * FFN_PAD, d), BF16),
                        pltpu.VMEM((tf // LANES, tm + 2 * FFN_PAD, LANES), F32),
                        pltpu.VMEM((tf // LANES, tm + 2 * FFN_PAD, LANES), F32)],
        compiler_params=_cparams(("parallel", "arbitrary")),
        name="conv_ffn",
    )(x1, x1, x1, norm2_g.reshape(1, d), w_up_bf, w_up_bf, ffn_dw_w, ffn_dw_w,
      ffn_dw_b.reshape(1, -1), ffn_dw_b.reshape(1, -1), w_down_bf)


def kernel(x, norm1_g, w_in, conv_dw_w, conv_dw_b, conv_ln_g, conv_ln_b, q_norm_g, k_norm_g,
           w_out, norm2_g, w_up, ffn_dw_w, ffn_dw_b, w_down):
    b, s, d = x.shape
    x2 = x.reshape(b * s, d)
    u = _in_proj(x2, norm1_g, w_in.astype(BF16), q_norm_g, k_norm_g)
    u3 = u.reshape(b, s, IN_COLS)
    a_out = _conv_group(u3, conv_dw_w, conv_dw_b, conv_ln_g, conv_ln_b)
    b_out = _attn_group(u3)
    x1 = _out_proj(x2, a_out.reshape(b * s, CONV_CH), b_out.reshape(b * s, ATTN_WIDTH),
                   w_out.astype(BF16))
    y = _conv_ffn(x1, norm2_g, w_up.astype(BF16), ffn_dw_w, ffn_dw_b, w_down.astype(BF16), s)
    return y.reshape(b, s, d)
```

```python
import functools

import jax
import jax.numpy as jnp
from jax import lax
from jax.experimental import pallas as pl
from jax.experimental.pallas import tpu as pltpu

D_MODEL = 2048
CONV_CH = 1024
ATTN_WIDTH = 1024
HEAD_DIM = 128
N_HEADS = ATTN_WIDTH // HEAD_DIM
CONV_WIDTH = 31
CONV_HALF = (CONV_WIDTH - 1) // 2
FFN_DIM = 5632
IN_COLS = 2 * CONV_CH + 3 * ATTN_WIDTH
RMS_EPS = 1e-6
LN_EPS = 1e-5
NEG_BIG = -1e30
BAND_R = 64
DILATIONS = (1, 4, 16)

LANES = 128
BF16_ROWS = 16
VMEM_LIMIT = 56 * 1024 * 1024

F32 = jnp.float32
BF16 = jnp.bfloat16


def _cparams(sem):
    return pltpu.CompilerParams(dimension_semantics=sem, vmem_limit_bytes=VMEM_LIMIT)


IN_TM = 1024
IN_TN = 512
Q_BLK0 = 2 * CONV_CH // IN_TN
K_BLK0 = Q_BLK0 + ATTN_WIDTH // IN_TN
V_BLK0 = K_BLK0 + ATTN_WIDTH // IN_TN
LOG2E = 1.4426950408889634


def _in_proj_kernel(x_ref, g_ref, w_ref, qg_ref, kg_ref, o_ref, h_scr):
    j = pl.program_id(1)

    @pl.when(j == 0)
    def _():
        xf = x_ref[...]
        ms = jnp.mean(xf * xf, axis=-1, keepdims=True)
        h_scr[...] = (xf * lax.rsqrt(ms + RMS_EPS) * g_ref[...]).astype(BF16)

    def proj():
        return jnp.dot(h_scr[...], w_ref[0], preferred_element_type=F32)

    def store_headnorm(gain):
        acc = proj()
        for hd in range(IN_TN // HEAD_DIM):
            ls = slice(hd * HEAD_DIM, (hd + 1) * HEAD_DIM)
            t = acc[:, ls]
            ms = jnp.mean(t * t, axis=-1, keepdims=True)
            o_ref[:, ls] = t * lax.rsqrt(ms + RMS_EPS) * gain

    @pl.when((j < Q_BLK0) | (j >= V_BLK0))
    def _():
        o_ref[...] = proj()

    @pl.when((j >= Q_BLK0) & (j < K_BLK0))
    def _():
        store_headnorm(qg_ref[...] * (HEAD_DIM ** -0.5 * LOG2E))

    @pl.when((j >= K_BLK0) & (j < V_BLK0))
    def _():
        store_headnorm(kg_ref[...])


def _in_proj(x2, g, w_blk, q_norm_g, k_norm_g):
    m, d = x2.shape
    tm, tn = IN_TM, IN_TN
    n = w_blk.shape[0] * tn
    gain = pl.BlockSpec((1, HEAD_DIM), lambda i, j: (0, 0))
    return pl.pallas_call(
        _in_proj_kernel,
        grid=(m // tm, n // tn),
        in_specs=[
            pl.BlockSpec((tm, d), lambda i, j: (i, 0)),
            pl.BlockSpec((1, d), lambda i, j: (0, 0)),
            pl.BlockSpec((1, d, tn), lambda i, j: (j, 0, 0)),
            gain, gain,
        ],
        out_specs=pl.BlockSpec((tm, tn), lambda i, j: (i, j)),
        out_shape=jax.ShapeDtypeStruct((m, n), F32),
        scratch_shapes=[pltpu.VMEM((tm, d), BF16)],
        compiler_params=_cparams(("parallel", "arbitrary")),
        name="in_proj",
    )(x2, g.reshape(1, d), w_blk, q_norm_g.reshape(1, HEAD_DIM), k_norm_g.reshape(1, HEAD_DIM))


CONV_TT = 256
CONV_HALO = 16
CONV_RC = 64
CONV_LB = CONV_CH // LANES


def _conv_kernel(val_ref, gate_ref, pval_ref, pgate_ref, nval_ref, ngate_ref,
                 w_ref, b_ref, lg_ref, lb_ref, o_ref, a_scr, y_scr):
    ti = pl.program_id(1)
    nt = pl.num_programs(1)
    tt = CONV_TT

    def glu(v, g):
        return v * jax.nn.sigmoid(g)

    keep_prev = (ti > 0).astype(F32)
    keep_next = (ti < nt - 1).astype(F32)
    for lb in range(CONV_LB):
        ls = slice(lb * LANES, (lb + 1) * LANES)
        a_scr[lb, 0:CONV_HALO, :] = glu(pval_ref[0, :, ls], pgate_ref[0, :, ls]) * keep_prev
        a_scr[lb, CONV_HALO:CONV_HALO + tt, :] = glu(val_ref[0, :, ls], gate_ref[0, :, ls])
        a_scr[lb, CONV_HALO + tt:, :] = glu(nval_ref[0, :, ls], ngate_ref[0, :, ls]) * keep_next

    def lane_block(lb, carry):
        for rc in range(tt // CONV_RC):
            r0 = rc * CONV_RC + CONV_HALO - CONV_HALF
            acc = jnp.broadcast_to(b_ref[lb], (CONV_RC, LANES))
            for k in range(CONV_WIDTH):
                acc = acc + a_scr[lb, r0 + k:r0 + k + CONV_RC, :] * w_ref[lb, k:k + 1, :]
            y_scr[lb, rc * CONV_RC:(rc + 1) * CONV_RC, :] = acc
        return carry

    lax.fori_loop(0, CONV_LB, lane_block, 0)

    tot = y_scr[0]
    for lb in range(1, CONV_LB):
        tot = tot + y_scr[lb]
    mu = jnp.sum(tot, axis=-1, keepdims=True) * (1.0 / CONV_CH)
    sq = None
    for lb in range(CONV_LB):
        c = y_scr[lb] - mu
        sq = c * c if sq is None else sq + c * c
    var = jnp.sum(sq, axis=-1, keepdims=True) * (1.0 / CONV_CH)
    rstd = lax.rsqrt(var + LN_EPS)
    for lb in range(CONV_LB):
        ls = slice(lb * LANES, (lb + 1) * LANES)
        z = (y_scr[lb] - mu) * rstd * lg_ref[:, ls] + lb_ref[:, ls]
        o_ref[0, :, ls] = (z * jax.nn.sigmoid(z)).astype(o_ref.dtype)


def _conv_group(u3, conv_dw_w, conv_dw_b, conv_ln_g, conv_ln_b):
    b, s, _ = u3.shape
    tt, halo = CONV_TT, CONV_HALO
    nt = s // tt
    hb = tt // halo
    n_hblk = s // halo
    w3 = conv_dw_w.reshape(CONV_WIDTH, CONV_LB, LANES).transpose(1, 0, 2)
    b3 = conv_dw_b.reshape(CONV_LB, 1, LANES)
    main = lambda col: pl.BlockSpec((1, tt, CONV_CH), lambda bi, ti: (bi, ti, col))
    prev = lambda col: pl.BlockSpec(
        (1, halo, CONV_CH), lambda bi, ti: (bi, jnp.maximum(ti * hb - 1, 0), col))
    nxt = lambda col: pl.BlockSpec(
        (1, halo, CONV_CH), lambda bi, ti: (bi, jnp.minimum((ti + 1) * hb, n_hblk - 1), col))
    full = lambda shape: pl.BlockSpec(shape, lambda bi, ti: (0,) * len(shape))
    return pl.pallas_call(
        _conv_kernel,
        grid=(b, nt),
        in_specs=[main(0), main(1), prev(0), prev(1), nxt(0), nxt(1),
                  full((CONV_LB, CONV_WIDTH, LANES)), full((CONV_LB, 1, LANES)),
                  full((1, CONV_CH)), full((1, CONV_CH))],
        out_specs=pl.BlockSpec((1, tt, CONV_CH), lambda bi, ti: (bi, ti, 0)),
        out_shape=jax.ShapeDtypeStruct((b, s, CONV_CH), BF16),
        scratch_shapes=[pltpu.VMEM((CONV_LB, tt + 2 * halo, LANES), F32),
                        pltpu.VMEM((CONV_LB, tt, LANES), F32)],
        compiler_params=_cparams(("parallel", "arbitrary")),
        name="conv_group",
    )(u3, u3, u3, u3, u3, u3, w3, b3, conv_ln_g.reshape(1, CONV_CH), conv_ln_b.reshape(1, CONV_CH))


ATT_QB = 128
ATT_KB = ATT_QB + 2 * BAND_R


ATT_C4 = 4


def _attn_kernel(slope_ref, q_ref, k_ref, v_ref, o_ref,
                 q4, k4, v4, bias_scr, acc_scr, m_scr, l_scr, out_scr):
    h = pl.program_id(1)
    s_len = q_ref.shape[1]
    cl = s_len // ATT_C4
    slope = slope_ref[h] * LOG2E

    for c4 in range(ATT_C4):
        dst = pl.ds(c4 * cl, cl)
        src = pl.ds(c4, cl, stride=ATT_C4)
        q4[dst, :] = q_ref[0, src, :]
        k4[dst, :] = k_ref[0, src, :]
        v4[dst, :] = v_ref[0, src, :]

    rows = lax.broadcasted_iota(jnp.int32, (ATT_QB, ATT_KB), 0)
    cols = lax.broadcasted_iota(jnp.int32, (ATT_QB, ATT_KB), 1)
    for w, dil in enumerate(DILATIONS):
        for e in range(3):
            off = jnp.abs(cols - rows - e * BAND_R)
            bias = jnp.where(off <= BAND_R, -(slope * dil) * off.astype(F32), NEG_BIG)
            bias_scr[3 * w + e] = bias

    ones_rhs = jnp.ones((ATT_KB, LANES), BF16)

    def block(w, qb, kb, vb, out_idx, bias):
        qb, kb, vb = qb.astype(BF16), kb.astype(BF16), vb.astype(BF16)
        sc = lax.dot_general(qb, kb, (((1,), (1,)), ((), ())), preferred_element_type=F32)
        sc = sc + bias
        m = jnp.max(sc, axis=-1, keepdims=True)
        p = jnp.exp2(sc - m).astype(BF16)
        acc = jnp.dot(p, jnp.concatenate([vb, ones_rhs[:vb.shape[0]]], axis=1),
                      preferred_element_type=F32)
        acc_scr[w, out_idx, :] = acc[:, :HEAD_DIM]
        l_scr[w, out_idx, :] = acc[:, HEAD_DIM:]
        m_scr[w, out_idx, :] = jnp.broadcast_to(m, (ATT_QB, LANES))

    def key_start(q0, class_len, nk):
        return min(max(q0 - BAND_R, 0), class_len - nk)

    for i in range(s_len // ATT_QB):
        q0 = i * ATT_QB
        k0 = key_start(q0, s_len, ATT_KB)
        qi, ki = pl.ds(q0, ATT_QB), pl.ds(k0, ATT_KB)
        block(0, q_ref[0, qi, :], k_ref[0, ki, :], v_ref[0, ki, :], qi,
              bias_scr[(q0 - k0) // BAND_R])

    for c4 in range(ATT_C4):
        for i in range(cl // ATT_QB):
            q0 = i * ATT_QB
            k0 = key_start(q0, cl, ATT_KB)
            qi, ki = pl.ds(c4 * cl + q0, ATT_QB), pl.ds(c4 * cl + k0, ATT_KB)
            block(1, q4[qi, :], k4[ki, :], v4[ki, :], qi, bias_scr[3 + (q0 - k0) // BAND_R])

    sub_len = s_len // DILATIONS[2]
    for c4 in range(ATT_C4):
        for c in range(DILATIONS[2] // ATT_C4):
            idx = pl.ds(c4 * cl + c, sub_len, stride=DILATIONS[2] // ATT_C4)
            block(2, q4[idx, :], k4[idx, :], v4[idx, :], idx, bias_scr[6][:, :sub_len])

    chunk = 256
    for c4 in range(ATT_C4):
        for r in range(cl // chunk):
            nat = pl.ds(c4 + ATT_C4 * r * chunk, chunk, stride=ATT_C4)
            grp = pl.ds(c4 * cl + r * chunk, chunk)
            m0, m1, m2 = m_scr[0, nat, :], m_scr[1, grp, :], m_scr[2, grp, :]
            mm = jnp.maximum(jnp.maximum(m0, m1), m2)
            a0, a1, a2 = jnp.exp2(m0 - mm), jnp.exp2(m1 - mm), jnp.exp2(m2 - mm)
            num = a0 * acc_scr[0, nat, :] + a1 * acc_scr[1, grp, :] + a2 * acc_scr[2, grp, :]
            den = a0 * l_scr[0, nat, :] + a1 * l_scr[1, grp, :] + a2 * l_scr[2, grp, :]
            out_scr[nat, :] = num / den
    o_ref[0] = out_scr[...].astype(o_ref.dtype)


def _attn_group(u3):
    b, s, _ = u3.shape
    qcol = 2 * CONV_CH // HEAD_DIM
    slopes = jnp.asarray([2.0 ** (-8.0 * (i + 1) / N_HEADS) for i in range(N_HEADS)], F32)
    head = lambda base: pl.BlockSpec((1, s, HEAD_DIM), lambda bi, hi: (bi, 0, base + hi))
    return pl.pallas_call(
        _attn_kernel,
        grid=(b, N_HEADS),
        in_specs=[pl.BlockSpec(memory_space=pltpu.SMEM),
                  head(qcol), head(qcol + N_HEADS), head(qcol + 2 * N_HEADS)],
        out_specs=pl.BlockSpec((1, s, HEAD_DIM), lambda bi, hi: (bi, 0, hi)),
        out_shape=jax.ShapeDtypeStruct((b, s, ATTN_WIDTH), BF16),
        scratch_shapes=[pltpu.VMEM((s, HEAD_DIM), F32),
                        pltpu.VMEM((s, HEAD_DIM), F32),
                        pltpu.VMEM((s, HEAD_DIM), F32),
                        pltpu.VMEM((9, ATT_QB, ATT_KB), F32),
                        pltpu.VMEM((3, s, HEAD_DIM), F32),
                        pltpu.VMEM((3, s, LANES), F32),
                        pltpu.VMEM((3, s, LANES), F32),
                        pltpu.VMEM((s, HEAD_DIM), F32)],
        compiler_params=_cparams(("parallel", "arbitrary")),
        name="attn_group",
    )(slopes, u3, u3, u3)


def _out_proj_kernel(x_ref, a_ref, b_ref, w_ref, g_ref, o_ref, h_ref):
    acc = jnp.dot(a_ref[...], w_ref[0:CONV_CH, :], preferred_element_type=F32)
    acc = acc + jnp.dot(b_ref[...], w_ref[CONV_CH:, :], preferred_element_type=F32)
    x1 = x_ref[...] + acc
    o_ref[...] = x1
    ms = jnp.mean(x1 * x1, axis=-1, keepdims=True)
    h_ref[...] = (x1 * lax.rsqrt(ms + RMS_EPS) * g_ref[...]).astype(BF16)


def _out_proj(x2, a2, b2, w_bf, norm2_g, tm=512):
    m, d = x2.shape
    return pl.pallas_call(
        _out_proj_kernel,
        grid=(m // tm,),
        in_specs=[
            pl.BlockSpec((tm, d), lambda i: (i, 0)),
            pl.BlockSpec((tm, CONV_CH), lambda i: (i, 0)),
            pl.BlockSpec((tm, ATTN_WIDTH), lambda i: (i, 0)),
            pl.BlockSpec((d, d), lambda i: (0, 0)),
            pl.BlockSpec((1, d), lambda i: (0, 0)),
        ],
        out_specs=[pl.BlockSpec((tm, d), lambda i: (i, 0)),
                   pl.BlockSpec((tm, d), lambda i: (i, 0))],
        out_shape=[jax.ShapeDtypeStruct((m, d), F32),
                   jax.ShapeDtypeStruct((m, d), BF16)],
        compiler_params=_cparams(("parallel",)),
        name="out_proj",
    )(x2, a2, b2, w_bf, norm2_g.reshape(1, d))


FFN_TM = 1024
FFN_TF = 512
FFN_PAD = BF16_ROWS
FFN_XC = 256
FFN_NXC = D_MODEL // FFN_XC


def _ffn_kernel(h_ref, hp_ref, hn_ref, x_ref, wg_ref, wv_ref, cwg_ref, cwv_ref,
                cbg_ref, cbv_ref, wd_ref, o_ref, h_scr, ug_scr, uv_scr, *, tiles_per_seq):
    i = pl.program_id(0)
    j = pl.program_id(1)
    tm = FFN_TM

    @pl.when(j == 0)
    def _():
        keep_prev = ((i % tiles_per_seq) != 0).astype(F32)
        keep_next = ((i % tiles_per_seq) != tiles_per_seq - 1).astype(F32)
        rid = lax.broadcasted_iota(jnp.int32, (FFN_PAD, 1), 0)
        hp = hp_ref[...].astype(F32)[FFN_PAD - 1:FFN_PAD, :] * keep_prev
        hn = hn_ref[...].astype(F32)[0:1, :] * keep_next
        h_scr[0:FFN_PAD, :] = jnp.where(rid == FFN_PAD - 1, hp, 0.0).astype(BF16)
        h_scr[FFN_PAD:FFN_PAD + tm, :] = h_ref[...]
        h_scr[FFN_PAD + tm:, :] = jnp.where(rid == 0, hn, 0.0).astype(BF16)
        o_ref[...] = jnp.zeros_like(o_ref)

    for c in range(FFN_NXC):
        @pl.when(j == c)
        def _(c=c):
            o_ref[:, c * FFN_XC:(c + 1) * FFN_XC] += x_ref[...]

    hh = h_scr[...]

    def up_conv(w_ref, cw_ref, cb_ref, u_scr):
        u = jnp.dot(hh, w_ref[0], preferred_element_type=F32)
        cols = []
        for c in range(FFN_TF // LANES):
            ls = slice(c * LANES, (c + 1) * LANES)
            u_scr[c] = u[:, ls]
            cols.append(cw_ref[0:1, ls] * u_scr[c, FFN_PAD - 1:FFN_PAD - 1 + tm, :]
                        + cw_ref[1:2, ls] * u_scr[c, FFN_PAD:FFN_PAD + tm, :]
                        + cw_ref[2:3, ls] * u_scr[c, FFN_PAD + 1:FFN_PAD + 1 + tm, :]
                        + cb_ref[:, ls])
        return jnp.concatenate(cols, axis=1)

    gte = up_conv(wg_ref, cwg_ref, cbg_ref, ug_scr)
    val = up_conv(wv_ref, cwv_ref, cbv_ref, uv_scr)
    act = (gte * jax.nn.sigmoid(gte) * val).astype(BF16)
    o_ref[...] += jnp.dot(act, wd_ref[...], preferred_element_type=F32)


def _conv_ffn(x1, h2, w_up_blk, ffn_dw_w, ffn_dw_b, w_down_bf, seq_len):
    m, d = x1.shape
    tm, tf = FFN_TM, FFN_TF
    nf = FFN_DIM // tf
    assert nf >= FFN_NXC
    hb = tm // FFN_PAD
    n_hblk = m // FFN_PAD
    kern = functools.partial(_ffn_kernel, tiles_per_seq=seq_len // tm)
    return pl.pallas_call(
        kern,
        grid=(m // tm, nf),
        in_specs=[
            pl.BlockSpec((tm, d), lambda i, j: (i, 0)),
            pl.BlockSpec((FFN_PAD, d), lambda i, j: (jnp.maximum(i * hb - 1, 0), 0)),
            pl.BlockSpec((FFN_PAD, d), lambda i, j: (jnp.minimum((i + 1) * hb, n_hblk - 1), 0)),
            pl.BlockSpec((tm, FFN_XC), lambda i, j: (i, jnp.minimum(j, FFN_NXC - 1))),
            pl.BlockSpec((1, d, tf), lambda i, j: (j, 0, 0)),
            pl.BlockSpec((1, d, tf), lambda i, j: (nf + j, 0, 0)),
            pl.BlockSpec((3, tf), lambda i, j: (0, j)),
            pl.BlockSpec((3, tf), lambda i, j: (0, nf + j)),
            pl.BlockSpec((1, tf), lambda i, j: (0, j)),
            pl.BlockSpec((1, tf), lambda i, j: (0, nf + j)),
            pl.BlockSpec((tf, d), lambda i, j: (j, 0)),
        ],
        out_specs=pl.BlockSpec((tm, d), lambda i, j: (i, 0)),
        out_shape=jax.ShapeDtypeStruct((m, d), F32),
        scratch_shapes=[pltpu.VMEM((tm + 2 * FFN_PAD, d), BF16),
                        pltpu.VMEM((tf // LANES, tm + 2 * FFN_PAD, LANES), F32),
                        pltpu.VMEM((tf // LANES, tm + 2 * FFN_PAD, LANES), F32)],
        compiler_params=_cparams(("parallel", "arbitrary")),
        name="conv_ffn",
    )(h2, h2, h2, x1, w_up_blk, w_up_blk, ffn_dw_w, ffn_dw_w,
      ffn_dw_b.reshape(1, -1), ffn_dw_b.reshape(1, -1), w_down_bf)


def _col_blocked_bf16(w, tn):
    k, n = w.shape
    return w.astype(BF16).reshape(k, n // tn, tn).transpose(1, 0, 2)


def kernel(x, norm1_g, w_in, conv_dw_w, conv_dw_b, conv_ln_g, conv_ln_b, q_norm_g, k_norm_g,
           w_out, norm2_g, w_up, ffn_dw_w, ffn_dw_b, w_down):
    b, s, d = x.shape
    x2 = x.reshape(b * s, d)
    u = _in_proj(x2, norm1_g, _col_blocked_bf16(w_in, IN_TN), q_norm_g, k_norm_g)
    u3 = u.reshape(b, s, IN_COLS)
    a_out = _conv_group(u3, conv_dw_w, conv_dw_b, conv_ln_g, conv_ln_b)
    b_out = _attn_group(u3)
    x1, h2 = _out_proj(x2, a_out.reshape(b * s, CONV_CH), b_out.reshape(b * s, ATTN_WIDTH),
                       w_out.astype(BF16), norm2_g)
    y = _conv_ffn(x1, h2, _col_blocked_bf16(w_up, FFN_TF), ffn_dw_w, ffn_dw_b,
                  w_down.astype(BF16), s)
    return y.reshape(b, s, d)
```

```python
import functools

import jax
import jax.numpy as jnp
from jax import lax
from jax.experimental import pallas as pl
from jax.experimental.pallas import tpu as pltpu

D_MODEL = 2048
CONV_CH = 1024
ATTN_WIDTH = 1024
HEAD_DIM = 128
N_HEADS = ATTN_WIDTH // HEAD_DIM
CONV_WIDTH = 31
CONV_HALF = (CONV_WIDTH - 1) // 2
FFN_DIM = 5632
IN_COLS = 2 * CONV_CH + 3 * ATTN_WIDTH
RMS_EPS = 1e-6
LN_EPS = 1e-5
NEG_BIG = -1e30
BAND_R = 64
DILATIONS = (1, 4, 16)

LANES = 128
BF16_ROWS = 16
VMEM_LIMIT = 56 * 1024 * 1024

F32 = jnp.float32
BF16 = jnp.bfloat16


def _cparams(sem):
    return pltpu.CompilerParams(dimension_semantics=sem, vmem_limit_bytes=VMEM_LIMIT)


IN_TM = 1024
IN_TN = 1024
Q_BLK0 = 2 * CONV_CH // IN_TN
K_BLK0 = Q_BLK0 + ATTN_WIDTH // IN_TN
V_BLK0 = K_BLK0 + ATTN_WIDTH // IN_TN
LOG2E = 1.4426950408889634


def _in_proj_kernel(x_ref, g_ref, w_ref, qg_ref, kg_ref, o_ref, h_scr):
    j = pl.program_id(1)

    @pl.when(j == 0)
    def _():
        xf = x_ref[...]
        ms = jnp.mean(xf * xf, axis=-1, keepdims=True)
        h_scr[...] = (xf * lax.rsqrt(ms + RMS_EPS) * g_ref[...]).astype(BF16)

    def proj():
        return jnp.dot(h_scr[...], w_ref[...], preferred_element_type=F32)

    def store_headnorm(gain):
        acc = proj()
        for hd in range(IN_TN // HEAD_DIM):
            ls = slice(hd * HEAD_DIM, (hd + 1) * HEAD_DIM)
            t = acc[:, ls]
            ms = jnp.mean(t * t, axis=-1, keepdims=True)
            o_ref[:, ls] = t * lax.rsqrt(ms + RMS_EPS) * gain

    @pl.when((j < Q_BLK0) | (j >= V_BLK0))
    def _():
        o_ref[...] = proj()

    @pl.when((j >= Q_BLK0) & (j < K_BLK0))
    def _():
        store_headnorm(qg_ref[...] * (HEAD_DIM ** -0.5 * LOG2E))

    @pl.when((j >= K_BLK0) & (j < V_BLK0))
    def _():
        store_headnorm(kg_ref[...])


def _in_proj(x2, g, w_bf, q_norm_g, k_norm_g):
    m, d = x2.shape
    tm, tn = IN_TM, IN_TN
    n = w_bf.shape[1]
    gain = pl.BlockSpec((1, HEAD_DIM), lambda i, j: (0, 0))
    return pl.pallas_call(
        _in_proj_kernel,
        grid=(m // tm, n // tn),
        in_specs=[
            pl.BlockSpec((tm, d), lambda i, j: (i, 0)),
            pl.BlockSpec((1, d), lambda i, j: (0, 0)),
            pl.BlockSpec((d, tn), lambda i, j: (0, j)),
            gain, gain,
        ],
        out_specs=pl.BlockSpec((tm, tn), lambda i, j: (i, j)),
        out_shape=jax.ShapeDtypeStruct((m, n), F32),
        scratch_shapes=[pltpu.VMEM((tm, d), BF16)],
        compiler_params=_cparams(("parallel", "arbitrary")),
        name="in_proj",
    )(x2, g.reshape(1, d), w_bf, q_norm_g.reshape(1, HEAD_DIM), k_norm_g.reshape(1, HEAD_DIM))


CONV_TT = 256
CONV_HALO = 16
CONV_RC = 64
CONV_LB = CONV_CH // LANES


def _conv_kernel(val_ref, gate_ref, pval_ref, pgate_ref, nval_ref, ngate_ref,
                 w_ref, b_ref, lg_ref, lb_ref, o_ref, a_scr, y_scr):
    ti = pl.program_id(1)
    nt = pl.num_programs(1)
    tt = CONV_TT

    def glu(v, g):
        return v * jax.nn.sigmoid(g)

    keep_prev = (ti > 0).astype(F32)
    keep_next = (ti < nt - 1).astype(F32)
    for lb in range(CONV_LB):
        ls = slice(lb * LANES, (lb + 1) * LANES)
        a_scr[lb, 0:CONV_HALO, :] = glu(pval_ref[0, :, ls], pgate_ref[0, :, ls]) * keep_prev
        a_scr[lb, CONV_HALO:CONV_HALO + tt, :] = glu(val_ref[0, :, ls], gate_ref[0, :, ls])
        a_scr[lb, CONV_HALO + tt:, :] = glu(nval_ref[0, :, ls], ngate_ref[0, :, ls]) * keep_next

    def lane_block(lb, carry):
        for rc in range(tt // CONV_RC):
            r0 = rc * CONV_RC + CONV_HALO - CONV_HALF
            acc = jnp.broadcast_to(b_ref[lb], (CONV_RC, LANES))
            for k in range(CONV_WIDTH):
                acc = acc + a_scr[lb, r0 + k:r0 + k + CONV_RC, :] * w_ref[lb, k:k + 1, :]
            y_scr[lb, rc * CONV_RC:(rc + 1) * CONV_RC, :] = acc
        return carry

    lax.fori_loop(0, CONV_LB, lane_block, 0)

    tot = y_scr[0]
    for lb in range(1, CONV_LB):
        tot = tot + y_scr[lb]
    mu = jnp.sum(tot, axis=-1, keepdims=True) * (1.0 / CONV_CH)
    sq = None
    for lb in range(CONV_LB):
        c = y_scr[lb] - mu
        sq = c * c if sq is None else sq + c * c
    var = jnp.sum(sq, axis=-1, keepdims=True) * (1.0 / CONV_CH)
    rstd = lax.rsqrt(var + LN_EPS)
    for lb in range(CONV_LB):
        ls = slice(lb * LANES, (lb + 1) * LANES)
        z = (y_scr[lb] - mu) * rstd * lg_ref[:, ls] + lb_ref[:, ls]
        o_ref[0, :, ls] = (z * jax.nn.sigmoid(z)).astype(o_ref.dtype)


def _conv_group(u3, conv_dw_w, conv_dw_b, conv_ln_g, conv_ln_b):
    b, s, _ = u3.shape
    tt, halo = CONV_TT, CONV_HALO
    nt = s // tt
    hb = tt // halo
    n_hblk = s // halo
    w3 = conv_dw_w.reshape(CONV_WIDTH, CONV_LB, LANES).transpose(1, 0, 2)
    b3 = conv_dw_b.reshape(CONV_LB, 1, LANES)
    main = lambda col: pl.BlockSpec((1, tt, CONV_CH), lambda bi, ti: (bi, ti, col))
    prev = lambda col: pl.BlockSpec(
        (1, halo, CONV_CH), lambda bi, ti: (bi, jnp.maximum(ti * hb - 1, 0), col))
    nxt = lambda col: pl.BlockSpec(
        (1, halo, CONV_CH), lambda bi, ti: (bi, jnp.minimum((ti + 1) * hb, n_hblk - 1), col))
    full = lambda shape: pl.BlockSpec(shape, lambda bi, ti: (0,) * len(shape))
    return pl.pallas_call(
        _conv_kernel,
        grid=(b, nt),
        in_specs=[main(0), main(1), prev(0), prev(1), nxt(0), nxt(1),
                  full((CONV_LB, CONV_WIDTH, LANES)), full((CONV_LB, 1, LANES)),
                  full((1, CONV_CH)), full((1, CONV_CH))],
        out_specs=pl.BlockSpec((1, tt, CONV_CH), lambda bi, ti: (bi, ti, 0)),
        out_shape=jax.ShapeDtypeStruct((b, s, CONV_CH), BF16),
        scratch_shapes=[pltpu.VMEM((CONV_LB, tt + 2 * halo, LANES), F32),
                        pltpu.VMEM((CONV_LB, tt, LANES), F32)],
        compiler_params=_cparams(("parallel", "arbitrary")),
        name="conv_group",
    )(u3, u3, u3, u3, u3, u3, w3, b3, conv_ln_g.reshape(1, CONV_CH), conv_ln_b.reshape(1, CONV_CH))


ATT_QB = 128
ATT_KB = ATT_QB + 2 * BAND_R


ATT_C4 = 4


def _attn_kernel(slope_ref, q_ref, k_ref, v_ref, o_ref,
                 q4, k4, v4, bias_scr, acc_scr, m_scr, l_scr, out_scr):
    h = pl.program_id(1)
    s_len = q_ref.shape[1]
    cl = s_len // ATT_C4
    slope = slope_ref[h] * LOG2E

    for c4 in range(ATT_C4):
        dst = pl.ds(c4 * cl, cl)
        src = pl.ds(c4, cl, stride=ATT_C4)
        q4[dst, :] = q_ref[0, src, :]
        k4[dst, :] = k_ref[0, src, :]
        v4[dst, :] = v_ref[0, src, :]

    rows = lax.broadcasted_iota(jnp.int32, (ATT_QB, ATT_KB), 0)
    cols = lax.broadcasted_iota(jnp.int32, (ATT_QB, ATT_KB), 1)
    for w, dil in enumerate(DILATIONS):
        for e in range(3):
            off = jnp.abs(cols - rows - e * BAND_R)
            bias = jnp.where(off <= BAND_R, -(slope * dil) * off.astype(F32), NEG_BIG)
            bias_scr[3 * w + e] = bias

    ones_rhs = jnp.ones((ATT_KB, LANES), BF16)

    def block(w, qb, kb, vb, out_idx, bias):
        qb, kb, vb = qb.astype(BF16), kb.astype(BF16), vb.astype(BF16)
        sc = lax.dot_general(qb, kb, (((1,), (1,)), ((), ())), preferred_element_type=F32)
        sc = sc + bias
        m = jnp.max(sc, axis=-1, keepdims=True)
        p = jnp.exp2(sc - m).astype(BF16)
        acc = jnp.dot(p, jnp.concatenate([vb, ones_rhs[:vb.shape[0]]], axis=1),
                      preferred_element_type=F32)
        acc_scr[w, out_idx, :] = acc[:, :HEAD_DIM]
        l_scr[w, out_idx, :] = acc[:, HEAD_DIM:]
        m_scr[w, out_idx, :] = jnp.broadcast_to(m, (ATT_QB, LANES))

    def key_start(q0, class_len, nk):
        return min(max(q0 - BAND_R, 0), class_len - nk)

    for i in range(s_len // ATT_QB):
        q0 = i * ATT_QB
        k0 = key_start(q0, s_len, ATT_KB)
        qi, ki = pl.ds(q0, ATT_QB), pl.ds(k0, ATT_KB)
        block(0, q_ref[0, qi, :], k_ref[0, ki, :], v_ref[0, ki, :], qi,
              bias_scr[(q0 - k0) // BAND_R])

    for c4 in range(ATT_C4):
        for i in range(cl // ATT_QB):
            q0 = i * ATT_QB
            k0 = key_start(q0, cl, ATT_KB)
            qi, ki = pl.ds(c4 * cl + q0, ATT_QB), pl.ds(c4 * cl + k0, ATT_KB)
            block(1, q4[qi, :], k4[ki, :], v4[ki, :], qi, bias_scr[3 + (q0 - k0) // BAND_R])

    sub_len = s_len // DILATIONS[2]
    for c4 in range(ATT_C4):
        for c in range(DILATIONS[2] // ATT_C4):
            idx = pl.ds(c4 * cl + c, sub_len, stride=DILATIONS[2] // ATT_C4)
            block(2, q4[idx, :], k4[idx, :], v4[idx, :], idx, bias_scr[6][:, :sub_len])

    chunk = 256
    for c4 in range(ATT_C4):
        for r in range(cl // chunk):
            nat = pl.ds(c4 + ATT_C4 * r * chunk, chunk, stride=ATT_C4)
            grp = pl.ds(c4 * cl + r * chunk, chunk)
            m0, m1, m2 = m_scr[0, nat, :], m_scr[1, grp, :], m_scr[2, grp, :]
            mm = jnp.maximum(jnp.maximum(m0, m1), m2)
            a0, a1, a2 = jnp.exp2(m0 - mm), jnp.exp2(m1 - mm), jnp.exp2(m2 - mm)
            num = a0 * acc_scr[0, nat, :] + a1 * acc_scr[1, grp, :] + a2 * acc_scr[2, grp, :]
            den = a0 * l_scr[0, nat, :] + a1 * l_scr[1, grp, :] + a2 * l_scr[2, grp, :]
            out_scr[nat, :] = num / den
    o_ref[0] = out_scr[...].astype(o_ref.dtype)


def _attn_group(u3):
    b, s, _ = u3.shape
    qcol = 2 * CONV_CH // HEAD_DIM
    slopes = jnp.asarray([2.0 ** (-8.0 * (i + 1) / N_HEADS) for i in range(N_HEADS)], F32)
    head = lambda base: pl.BlockSpec((1, s, HEAD_DIM), lambda bi, hi: (bi, 0, base + hi))
    return pl.pallas_call(
        _attn_kernel,
        grid=(b, N_HEADS),
        in_specs=[pl.BlockSpec(memory_space=pltpu.SMEM),
                  head(qcol), head(qcol + N_HEADS), head(qcol + 2 * N_HEADS)],
        out_specs=pl.BlockSpec((1, s, HEAD_DIM), lambda bi, hi: (bi, 0, hi)),
        out_shape=jax.ShapeDtypeStruct((b, s, ATTN_WIDTH), BF16),
        scratch_shapes=[pltpu.VMEM((s, HEAD_DIM), F32),
                        pltpu.VMEM((s, HEAD_DIM), F32),
                        pltpu.VMEM((s, HEAD_DIM), F32),
                        pltpu.VMEM((9, ATT_QB, ATT_KB), F32),
                        pltpu.VMEM((3, s, HEAD_DIM), F32),
                        pltpu.VMEM((3, s, LANES), F32),
                        pltpu.VMEM((3, s, LANES), F32),
                        pltpu.VMEM((s, HEAD_DIM), F32)],
        compiler_params=_cparams(("parallel", "arbitrary")),
        name="attn_group",
    )(slopes, u3, u3, u3)


def _out_proj_kernel(x_ref, a_ref, b_ref, w_ref, g_ref, o_ref, h_ref):
    acc = jnp.dot(a_ref[...], w_ref[0:CONV_CH, :], preferred_element_type=F32)
    acc = acc + jnp.dot(b_ref[...], w_ref[CONV_CH:, :], preferred_element_type=F32)
    x1 = x_ref[...] + acc
    o_ref[...] = x1
    ms = jnp.mean(x1 * x1, axis=-1, keepdims=True)
    h_ref[...] = (x1 * lax.rsqrt(ms + RMS_EPS) * g_ref[...]).astype(BF16)


def _out_proj(x2, a2, b2, w_bf, norm2_g, tm=512):
    m, d = x2.shape
    return pl.pallas_call(
        _out_proj_kernel,
        grid=(m // tm,),
        in_specs=[
            pl.BlockSpec((tm, d), lambda i: (i, 0)),
            pl.BlockSpec((tm, CONV_CH), lambda i: (i, 0)),
            pl.BlockSpec((tm, ATTN_WIDTH), lambda i: (i, 0)),
            pl.BlockSpec((d, d), lambda i: (0, 0)),
            pl.BlockSpec((1, d), lambda i: (0, 0)),
        ],
        out_specs=[pl.BlockSpec((tm, d), lambda i: (i, 0)),
                   pl.BlockSpec((tm, d), lambda i: (i, 0))],
        out_shape=[jax.ShapeDtypeStruct((m, d), F32),
                   jax.ShapeDtypeStruct((m, d), BF16)],
        compiler_params=_cparams(("parallel",)),
        name="out_proj",
    )(x2, a2, b2, w_bf, norm2_g.reshape(1, d))


FFN_TM = 1024
FFN_TF = 512
FFN_PAD = BF16_ROWS
FFN_XC = 256
FFN_NXC = D_MODEL // FFN_XC


def _ffn_kernel(h_ref, hp_ref, hn_ref, x_ref, wg_ref, wv_ref, cwg_ref, cwv_ref,
                cbg_ref, cbv_ref, wd_ref, o_ref, h_scr, ug_scr, uv_scr, *, tiles_per_seq):
    i = pl.program_id(0)
    j = pl.program_id(1)
    tm = FFN_TM

    @pl.when(j == 0)
    def _():
        keep_prev = ((i % tiles_per_seq) != 0).astype(F32)
        keep_next = ((i % tiles_per_seq) != tiles_per_seq - 1).astype(F32)
        rid = lax.broadcasted_iota(jnp.int32, (FFN_PAD, 1), 0)
        hp = hp_ref[...].astype(F32)[FFN_PAD - 1:FFN_PAD, :] * keep_prev
        hn = hn_ref[...].astype(F32)[0:1, :] * keep_next
        h_scr[0:FFN_PAD, :] = jnp.where(rid == FFN_PAD - 1, hp, 0.0).astype(BF16)
        h_scr[FFN_PAD:FFN_PAD + tm, :] = h_ref[...]
        h_scr[FFN_PAD + tm:, :] = jnp.where(rid == 0, hn, 0.0).astype(BF16)
        o_ref[...] = jnp.zeros_like(o_ref)

    for c in range(FFN_NXC):
        @pl.when(j == c)
        def _(c=c):
            o_ref[:, c * FFN_XC:(c + 1) * FFN_XC] += x_ref[...]

    hh = h_scr[...]

    def up_conv(w_ref, cw_ref, cb_ref, u_scr):
        u = jnp.dot(hh, w_ref[...], preferred_element_type=F32)
        cols = []
        for c in range(FFN_TF // LANES):
            ls = slice(c * LANES, (c + 1) * LANES)
            u_scr[c] = u[:, ls]
            cols.append(cw_ref[0:1, ls] * u_scr[c, FFN_PAD - 1:FFN_PAD - 1 + tm, :]
                        + cw_ref[1:2, ls] * u_scr[c, FFN_PAD:FFN_PAD + tm, :]
                        + cw_ref[2:3, ls] * u_scr[c, FFN_PAD + 1:FFN_PAD + 1 + tm, :]
                        + cb_ref[:, ls])
        return jnp.concatenate(cols, axis=1)

    gte = up_conv(wg_ref, cwg_ref, cbg_ref, ug_scr)
    val = up_conv(wv_ref, cwv_ref, cbv_ref, uv_scr)
    act = (gte * jax.nn.sigmoid(gte) * val).astype(BF16)
    o_ref[...] += jnp.dot(act, wd_ref[...], preferred_element_type=F32)


def _conv_ffn(x1, h2, w_up_bf, ffn_dw_w, ffn_dw_b, w_down_bf, seq_len):
    m, d = x1.shape
    tm, tf = FFN_TM, FFN_TF
    nf = FFN_DIM // tf
    assert nf >= FFN_NXC
    hb = tm // FFN_PAD
    n_hblk = m // FFN_PAD
    kern = functools.partial(_ffn_kernel, tiles_per_seq=seq_len // tm)
    return pl.pallas_call(
        kern,
        grid=(m // tm, nf),
        in_specs=[
            pl.BlockSpec((tm, d), lambda i, j: (i, 0)),
            pl.BlockSpec((FFN_PAD, d), lambda i, j: (jnp.maximum(i * hb - 1, 0), 0)),
            pl.BlockSpec((FFN_PAD, d), lambda i, j: (jnp.minimum((i + 1) * hb, n_hblk - 1), 0)),
            pl.BlockSpec((tm, FFN_XC), lambda i, j: (i, jnp.minimum(j, FFN_NXC - 1))),
            pl.BlockSpec((d, tf), lambda i, j: (0, j)),
            pl.BlockSpec((d, tf), lambda i, j: (0, nf + j)),
            pl.BlockSpec((3, tf), lambda i, j: (0, j)),
            pl.BlockSpec((3, tf), lambda i, j: (0, nf + j)),
            pl.BlockSpec((1, tf), lambda i, j: (0, j)),
            pl.BlockSpec((1, tf), lambda i, j: (0, nf + j)),
            pl.BlockSpec((tf, d), lambda i, j: (j, 0)),
        ],
        out_specs=pl.BlockSpec((tm, d), lambda i, j: (i, 0)),
        out_shape=jax.ShapeDtypeStruct((m, d), F32),
        scratch_shapes=[pltpu.VMEM((tm + 2 * FFN_PAD, d), BF16),
                        pltpu.VMEM((tf // LANES, tm + 2 * FFN_PAD, LANES), F32),
                        pltpu.VMEM((tf // LANES, tm + 2 * FFN_PAD, LANES), F32)],
        compiler_params=_cparams(("parallel", "arbitrary")),
        name="conv_ffn",
    )(h2, h2, h2, x1, w_up_bf, w_up_bf, ffn_dw_w, ffn_dw_w,
      ffn_dw_b.reshape(1, -1), ffn_dw_b.reshape(1, -1), w_down_bf)


def kernel(x, norm1_g, w_in, conv_dw_w, conv_dw_b, conv_ln_g, conv_ln_b, q_norm_g, k_norm_g,
           w_out, norm2_g, w_up, ffn_dw_w, ffn_dw_b, w_down):
    b, s, d = x.shape
    x2 = x.reshape(b * s, d)
    u = _in_proj(x2, norm1_g, w_in.astype(BF16), q_norm_g, k_norm_g)
    u3 = u.reshape(b, s, IN_COLS)
    a_out = _conv_group(u3, conv_dw_w, conv_dw_b, conv_ln_g, conv_ln_b)
    b_out = _attn_group(u3)
    x1, h2 = _out_proj(x2, a_out.reshape(b * s, CONV_CH), b_out.reshape(b * s, ATTN_WIDTH),
                       w_out.astype(BF16), norm2_g)
    y = _conv_ffn(x1, h2, w_up.astype(BF16), ffn_dw_w, ffn_dw_b, w_down.astype(BF16), s)
    return y.reshape(b, s, d)
```

```python
import functools

import jax
import jax.numpy as jnp
from jax import lax
from jax.experimental import pallas as pl
from jax.experimental.pallas import tpu as pltpu

D_MODEL = 2048
CONV_CH = 1024
ATTN_WIDTH = 1024
HEAD_DIM = 128
N_HEADS = ATTN_WIDTH // HEAD_DIM
CONV_WIDTH = 31
CONV_HALF = (CONV_WIDTH - 1) // 2
FFN_DIM = 5632
RMS_EPS = 1e-6
LN_EPS = 1e-5
NEG_BIG = -1e30
BAND_R = 64
DILATIONS = (1, 4, 16)
LOG2E = 1.4426950408889634

LANES = 128
BF16_ROWS = 16
VMEM_LIMIT = 56 * 1024 * 1024

F32 = jnp.float32
BF16 = jnp.bfloat16


def _cparams(sem):
    return pltpu.CompilerParams(dimension_semantics=sem, vmem_limit_bytes=VMEM_LIMIT)


IN_TM = 1024
IN_TN = 1024
assert CONV_CH == IN_TN and ATTN_WIDTH == IN_TN
STEP_VAL, STEP_GATE, STEP_Q, STEP_K, STEP_V = range(5)
U_COLS = CONV_CH + 3 * ATTN_WIDTH


def _in_proj_kernel(x_ref, g_ref, w_ref, qg_ref, kg_ref, o_ref, h_scr):
    j = pl.program_id(1)

    @pl.when(j == 0)
    def _():
        xf = x_ref[...]
        ms = jnp.mean(xf * xf, axis=-1, keepdims=True)
        h_scr[...] = (xf * lax.rsqrt(ms + RMS_EPS) * g_ref[...]).astype(BF16)

    def proj():
        return jnp.dot(h_scr[...], w_ref[...], preferred_element_type=F32)

    def store_headnorm(gain):
        acc = proj()
        for hd in range(IN_TN // HEAD_DIM):
            ls = slice(hd * HEAD_DIM, (hd + 1) * HEAD_DIM)
            t = acc[:, ls]
            ms = jnp.mean(t * t, axis=-1, keepdims=True)
            o_ref[:, ls] = t * lax.rsqrt(ms + RMS_EPS) * gain

    @pl.when((j == STEP_VAL) | (j == STEP_V))
    def _():
        o_ref[...] = proj()

    @pl.when(j == STEP_GATE)
    def _():
        o_ref[...] = o_ref[...] * jax.nn.sigmoid(proj())

    @pl.when(j == STEP_Q)
    def _():
        store_headnorm(qg_ref[...] * (HEAD_DIM ** -0.5 * LOG2E))

    @pl.when(j == STEP_K)
    def _():
        store_headnorm(kg_ref[...])


def _in_proj(x2, g, w_bf, q_norm_g, k_norm_g):
    m, d = x2.shape
    tm, tn = IN_TM, IN_TN
    gain = pl.BlockSpec((1, HEAD_DIM), lambda i, j: (0, 0))
    return pl.pallas_call(
        _in_proj_kernel,
        grid=(m // tm, w_bf.shape[1] // tn),
        in_specs=[
            pl.BlockSpec((tm, d), lambda i, j: (i, 0)),
            pl.BlockSpec((1, d), lambda i, j: (0, 0)),
            pl.BlockSpec((d, tn), lambda i, j: (0, j)),
            gain, gain,
        ],
        out_specs=pl.BlockSpec((tm, tn), lambda i, j: (i, jnp.maximum(j - 1, 0))),
        out_shape=jax.ShapeDtypeStruct((m, U_COLS), F32),
        scratch_shapes=[pltpu.VMEM((tm, d), BF16)],
        compiler_params=_cparams(("parallel", "arbitrary")),
        name="in_proj",
    )(x2, g.reshape(1, d), w_bf, q_norm_g.reshape(1, HEAD_DIM), k_norm_g.reshape(1, HEAD_DIM))


CONV_TT = 256
CONV_HALO = 16
CONV_RC = 64
CONV_LB = CONV_CH // LANES


def _conv_kernel(a_ref, pa_ref, na_ref, w_ref, b_ref, o_ref, a_scr):
    ti = pl.program_id(1)
    nt = pl.num_programs(1)
    tt = CONV_TT

    keep_prev = (ti > 0).astype(F32)
    keep_next = (ti < nt - 1).astype(F32)
    for lb in range(CONV_LB):
        ls = slice(lb * LANES, (lb + 1) * LANES)
        a_scr[lb, 0:CONV_HALO, :] = pa_ref[0, :, ls] * keep_prev
        a_scr[lb, CONV_HALO:CONV_HALO + tt, :] = a_ref[0, :, ls]
        a_scr[lb, CONV_HALO + tt:, :] = na_ref[0, :, ls] * keep_next

    def lane_block(lb, carry):
        for rc in range(tt // CONV_RC):
            r0 = rc * CONV_RC + CONV_HALO - CONV_HALF
            acc = jnp.broadcast_to(b_ref[lb], (CONV_RC, LANES))
            for k in range(CONV_WIDTH):
                acc = acc + a_scr[lb, r0 + k:r0 + k + CONV_RC, :] * w_ref[lb, k:k + 1, :]
            o_ref[lb, 0, rc * CONV_RC:(rc + 1) * CONV_RC, :] = acc
        return carry

    lax.fori_loop(0, CONV_LB, lane_block, 0)


def _conv_group(u3, conv_dw_w, conv_dw_b):
    b, s, _ = u3.shape
    tt, halo = CONV_TT, CONV_HALO
    nt = s // tt
    hb = tt // halo
    n_hblk = s // halo
    w3 = conv_dw_w.reshape(CONV_WIDTH, CONV_LB, LANES).transpose(1, 0, 2)
    b3 = conv_dw_b.reshape(CONV_LB, 1, LANES)
    full = lambda shape: pl.BlockSpec(shape, lambda bi, ti: (0,) * len(shape))
    return pl.pallas_call(
        _conv_kernel,
        grid=(b, nt),
        in_specs=[
            pl.BlockSpec((1, tt, CONV_CH), lambda bi, ti: (bi, ti, 0)),
            pl.BlockSpec((1, halo, CONV_CH), lambda bi, ti: (bi, jnp.maximum(ti * hb - 1, 0), 0)),
            pl.BlockSpec((1, halo, CONV_CH),
                         lambda bi, ti: (bi, jnp.minimum((ti + 1) * hb, n_hblk - 1), 0)),
            full((CONV_LB, CONV_WIDTH, LANES)), full((CONV_LB, 1, LANES))],
        out_specs=pl.BlockSpec((CONV_LB, 1, tt, LANES), lambda bi, ti: (0, bi, ti, 0)),
        out_shape=jax.ShapeDtypeStruct((CONV_LB, b, s, LANES), F32),
        scratch_shapes=[pltpu.VMEM((CONV_LB, tt + 2 * halo, LANES), F32)],
        compiler_params=_cparams(("parallel", "arbitrary")),
        name="conv_group",
    )(u3, u3, u3, w3, b3)


ATT_QB = 128
ATT_KB = ATT_QB + 2 * BAND_R
ATT_C4 = 4


def _attn_kernel(slope_ref, q_ref, k_ref, v_ref, o_ref,
                 q4, k4, v4, bias_scr, acc_scr, m_scr, l_scr, out_scr):
    h = pl.program_id(1)
    s_len = q_ref.shape[1]
    cl = s_len // ATT_C4
    slope = slope_ref[h] * LOG2E

    for c4 in range(ATT_C4):
        dst = pl.ds(c4 * cl, cl)
        src = pl.ds(c4, cl, stride=ATT_C4)
        q4[dst, :] = q_ref[0, src, :]
        k4[dst, :] = k_ref[0, src, :]
        v4[dst, :] = v_ref[0, src, :]

    rows = lax.broadcasted_iota(jnp.int32, (ATT_QB, ATT_KB), 0)
    cols = lax.broadcasted_iota(jnp.int32, (ATT_QB, ATT_KB), 1)
    for w, dil in enumerate(DILATIONS):
        for e in range(3):
            off = jnp.abs(cols - rows - e * BAND_R)
            bias = jnp.where(off <= BAND_R, -(slope * dil) * off.astype(F32), NEG_BIG)
            bias_scr[3 * w + e] = bias

    ones_rhs = jnp.ones((ATT_KB, LANES), BF16)

    def block(w, qb, kb, vb, out_idx, bias):
        qb, kb, vb = qb.astype(BF16), kb.astype(BF16), vb.astype(BF16)
        sc = lax.dot_general(qb, kb, (((1,), (1,)), ((), ())), preferred_element_type=F32)
        sc = sc + bias
        m = jnp.max(sc, axis=-1, keepdims=True)
        p = jnp.exp2(sc - m).astype(BF16)
        acc = jnp.dot(p, jnp.concatenate([vb, ones_rhs[:vb.shape[0]]], axis=1),
                      preferred_element_type=F32)
        acc_scr[w, out_idx, :] = acc[:, :HEAD_DIM]
        l_scr[w, out_idx, :] = acc[:, HEAD_DIM:]
        m_scr[w, out_idx, :] = jnp.broadcast_to(m, (ATT_QB, LANES))

    def key_start(q0, class_len, nk):
        return min(max(q0 - BAND_R, 0), class_len - nk)

    for i in range(s_len // ATT_QB):
        q0 = i * ATT_QB
        k0 = key_start(q0, s_len, ATT_KB)
        qi, ki = pl.ds(q0, ATT_QB), pl.ds(k0, ATT_KB)
        block(0, q_ref[0, qi, :], k_ref[0, ki, :], v_ref[0, ki, :], qi,
              bias_scr[(q0 - k0) // BAND_R])

    for c4 in range(ATT_C4):
        for i in range(cl // ATT_QB):
            q0 = i * ATT_QB
            k0 = key_start(q0, cl, ATT_KB)
            qi, ki = pl.ds(c4 * cl + q0, ATT_QB), pl.ds(c4 * cl + k0, ATT_KB)
            block(1, q4[qi, :], k4[ki, :], v4[ki, :], qi, bias_scr[3 + (q0 - k0) // BAND_R])

    sub_len = s_len // DILATIONS[2]
    for c4 in range(ATT_C4):
        for c in range(DILATIONS[2] // ATT_C4):
            idx = pl.ds(c4 * cl + c, sub_len, stride=DILATIONS[2] // ATT_C4)
            block(2, q4[idx, :], k4[idx, :], v4[idx, :], idx, bias_scr[6][:, :sub_len])

    chunk = 256
    for c4 in range(ATT_C4):
        for r in range(cl // chunk):
            nat = pl.ds(c4 + ATT_C4 * r * chunk, chunk, stride=ATT_C4)
            grp = pl.ds(c4 * cl + r * chunk, chunk)
            m0, m1, m2 = m_scr[0, nat, :], m_scr[1, grp, :], m_scr[2, grp, :]
            mm = jnp.maximum(jnp.maximum(m0, m1), m2)
            a0, a1, a2 = jnp.exp2(m0 - mm), jnp.exp2(m1 - mm), jnp.exp2(m2 - mm)
            num = a0 * acc_scr[0, nat, :] + a1 * acc_scr[1, grp, :] + a2 * acc_scr[2, grp, :]
            den = a0 * l_scr[0, nat, :] + a1 * l_scr[1, grp, :] + a2 * l_scr[2, grp, :]
            out_scr[nat, :] = num / den
    o_ref[0] = out_scr[...].astype(o_ref.dtype)


def _attn_group(u3):
    b, s, _ = u3.shape
    qcol = CONV_CH // HEAD_DIM
    slopes = jnp.asarray([2.0 ** (-8.0 * (i + 1) / N_HEADS) for i in range(N_HEADS)], F32)
    head = lambda base: pl.BlockSpec((1, s, HEAD_DIM), lambda bi, hi: (bi, 0, base + hi))
    return pl.pallas_call(
        _attn_kernel,
        grid=(b, N_HEADS),
        in_specs=[pl.BlockSpec(memory_space=pltpu.SMEM),
                  head(qcol), head(qcol + N_HEADS), head(qcol + 2 * N_HEADS)],
        out_specs=pl.BlockSpec((1, s, HEAD_DIM), lambda bi, hi: (bi, 0, hi)),
        out_shape=jax.ShapeDtypeStruct((b, s, ATTN_WIDTH), BF16),
        scratch_shapes=[pltpu.VMEM((s, HEAD_DIM), F32),
                        pltpu.VMEM((s, HEAD_DIM), F32),
                        pltpu.VMEM((s, HEAD_DIM), F32),
                        pltpu.VMEM((9, ATT_QB, ATT_KB), F32),
                        pltpu.VMEM((3, s, HEAD_DIM), F32),
                        pltpu.VMEM((3, s, LANES), F32),
                        pltpu.VMEM((3, s, LANES), F32),
                        pltpu.VMEM((s, HEAD_DIM), F32)],
        compiler_params=_cparams(("parallel", "arbitrary")),
        name="attn_group",
    )(slopes, u3, u3, u3)


def _out_proj_kernel(x_ref, y_ref, b_ref, w_ref, lg_ref, lb_ref, g_ref, o_ref, h_ref):
    acc = jnp.dot(b_ref[...], w_ref[CONV_CH:, :], preferred_element_type=F32)

    ys = [y_ref[lb] for lb in range(CONV_LB)]
    tot = ys[0]
    for t in ys[1:]:
        tot = tot + t
    mu = jnp.sum(tot, axis=-1, keepdims=True) * (1.0 / CONV_CH)
    sq = None
    for t in ys:
        c = t - mu
        sq = c * c if sq is None else sq + c * c
    var = jnp.sum(sq, axis=-1, keepdims=True) * (1.0 / CONV_CH)
    rstd = lax.rsqrt(var + LN_EPS)
    a_cols = []
    for lb, t in enumerate(ys):
        ls = slice(lb * LANES, (lb + 1) * LANES)
        z = (t - mu) * rstd * lg_ref[:, ls] + lb_ref[:, ls]
        a_cols.append((z * jax.nn.sigmoid(z)).astype(BF16))
    a = jnp.concatenate(a_cols, axis=1)

    acc = acc + jnp.dot(a, w_ref[0:CONV_CH, :], preferred_element_type=F32)
    x1 = x_ref[...] + acc
    o_ref[...] = x1
    ms = jnp.mean(x1 * x1, axis=-1, keepdims=True)
    h_ref[...] = (x1 * lax.rsqrt(ms + RMS_EPS) * g_ref[...]).astype(BF16)


def _out_proj(x2, y_slabs, b2, w_bf, conv_ln_g, conv_ln_b, norm2_g, tm=512):
    m, d = x2.shape
    row = lambda shape: pl.BlockSpec(shape, lambda i: (0,) * len(shape))
    return pl.pallas_call(
        _out_proj_kernel,
        grid=(m // tm,),
        in_specs=[
            pl.BlockSpec((tm, d), lambda i: (i, 0)),
            pl.BlockSpec((CONV_LB, tm, LANES), lambda i: (0, i, 0)),
            pl.BlockSpec((tm, ATTN_WIDTH), lambda i: (i, 0)),
            row((d, d)), row((1, CONV_CH)), row((1, CONV_CH)), row((1, d)),
        ],
        out_specs=[pl.BlockSpec((tm, d), lambda i: (i, 0)),
                   pl.BlockSpec((tm, d), lambda i: (i, 0))],
        out_shape=[jax.ShapeDtypeStruct((m, d), F32),
                   jax.ShapeDtypeStruct((m, d), BF16)],
        compiler_params=_cparams(("parallel",)),
        name="out_proj",
    )(x2, y_slabs, b2, w_bf, conv_ln_g.reshape(1, CONV_CH), conv_ln_b.reshape(1, CONV_CH),
      norm2_g.reshape(1, d))


FFN_TM = 1024
FFN_TF = 512
FFN_PAD = BF16_ROWS
FFN_XC = 256
FFN_NXC = D_MODEL // FFN_XC


def _ffn_kernel(h_ref, hp_ref, hn_ref, x_ref, wg_ref, wv_ref, cwg_ref, cwv_ref,
                cbg_ref, cbv_ref, wd_ref, o_ref, h_scr, ug_scr, uv_scr, *, tiles_per_seq):
    i = pl.program_id(0)
    j = pl.program_id(1)
    tm = FFN_TM

    @pl.when(j == 0)
    def _():
        keep_prev = ((i % tiles_per_seq) != 0).astype(F32)
        keep_next = ((i % tiles_per_seq) != tiles_per_seq - 1).astype(F32)
        rid = lax.broadcasted_iota(jnp.int32, (FFN_PAD, 1), 0)
        hp = hp_ref[...].astype(F32)[FFN_PAD - 1:FFN_PAD, :] * keep_prev
        hn = hn_ref[...].astype(F32)[0:1, :] * keep_next
        h_scr[0:FFN_PAD, :] = jnp.where(rid == FFN_PAD - 1, hp, 0.0).astype(BF16)
        h_scr[FFN_PAD:FFN_PAD + tm, :] = h_ref[...]
        h_scr[FFN_PAD + tm:, :] = jnp.where(rid == 0, hn, 0.0).astype(BF16)
        o_ref[...] = jnp.zeros_like(o_ref)

    for c in range(FFN_NXC):
        @pl.when(j == c)
        def _(c=c):
            o_ref[:, c * FFN_XC:(c + 1) * FFN_XC] += x_ref[...]

    hh = h_scr[...]

    def up_conv(w_ref, cw_ref, cb_ref, u_scr):
        u = jnp.dot(hh, w_ref[...], preferred_element_type=F32)
        cols = []
        for c in range(FFN_TF // LANES):
            ls = slice(c * LANES, (c + 1) * LANES)
            u_scr[c] = u[:, ls]
            cols.append(cw_ref[0:1, ls] * u_scr[c, FFN_PAD - 1:FFN_PAD - 1 + tm, :]
                        + cw_ref[1:2, ls] * u_scr[c, FFN_PAD:FFN_PAD + tm, :]
                        + cw_ref[2:3, ls] * u_scr[c, FFN_PAD + 1:FFN_PAD + 1 + tm, :]
                        + cb_ref[:, ls])
        return jnp.concatenate(cols, axis=1)

    gte = up_conv(wg_ref, cwg_ref, cbg_ref, ug_scr)
    val = up_conv(wv_ref, cwv_ref, cbv_ref, uv_scr)
    act = (gte * jax.nn.sigmoid(gte) * val).astype(BF16)
    o_ref[...] += jnp.dot(act, wd_ref[...], preferred_element_type=F32)


def _conv_ffn(x1, h2, w_up_bf, ffn_dw_w, ffn_dw_b, w_down_bf, seq_len):
    m, d = x1.shape
    tm, tf = FFN_TM, FFN_TF
    nf = FFN_DIM // tf
    assert nf >= FFN_NXC
    hb = tm // FFN_PAD
    n_hblk = m // FFN_PAD
    kern = functools.partial(_ffn_kernel, tiles_per_seq=seq_len // tm)
    return pl.pallas_call(
        kern,
        grid=(m // tm, nf),
        in_specs=[
            pl.BlockSpec((tm, d), lambda i, j: (i, 0)),
            pl.BlockSpec((FFN_PAD, d), lambda i, j: (jnp.maximum(i * hb - 1, 0), 0)),
            pl.BlockSpec((FFN_PAD, d), lambda i, j: (jnp.minimum((i + 1) * hb, n_hblk - 1), 0)),
            pl.BlockSpec((tm, FFN_XC), lambda i, j: (i, jnp.minimum(j, FFN_NXC - 1))),
            pl.BlockSpec((d, tf), lambda i, j: (0, j)),
            pl.BlockSpec((d, tf), lambda i, j: (0, nf + j)),
            pl.BlockSpec((3, tf), lambda i, j: (0, j)),
            pl.BlockSpec((3, tf), lambda i, j: (0, nf + j)),
            pl.BlockSpec((1, tf), lambda i, j: (0, j)),
            pl.BlockSpec((1, tf), lambda i, j: (0, nf + j)),
            pl.BlockSpec((tf, d), lambda i, j: (j, 0)),
        ],
        out_specs=pl.BlockSpec((tm, d), lambda i, j: (i, 0)),
        out_shape=jax.ShapeDtypeStruct((m, d), F32),
        scratch_shapes=[pltpu.VMEM((tm + 2 * FFN_PAD, d), BF16),
                        pltpu.VMEM((tf // LANES, tm + 2 * FFN_PAD, LANES), F32),
                        pltpu.VMEM((tf // LANES, tm + 2 * FFN_PAD, LANES), F32)],
        compiler_params=_cparams(("parallel", "arbitrary")),
        name="conv_ffn",
    )(h2, h2, h2, x1, w_up_bf, w_up_bf, ffn_dw_w, ffn_dw_w,
      ffn_dw_b.reshape(1, -1), ffn_dw_b.reshape(1, -1), w_down_bf)


def kernel(x, norm1_g, w_in, conv_dw_w, conv_dw_b, conv_ln_g, conv_ln_b, q_norm_g, k_norm_g,
           w_out, norm2_g, w_up, ffn_dw_w, ffn_dw_b, w_down):
    b, s, d = x.shape
    x2 = x.reshape(b * s, d)
    u = _in_proj(x2, norm1_g, w_in.astype(BF16), q_norm_g, k_norm_g)
    u3 = u.reshape(b, s, U_COLS)
    y_slabs = _conv_group(u3, conv_dw_w, conv_dw_b)
    b_out = _attn_group(u3)
    x1, h2 = _out_proj(x2, y_slabs.reshape(CONV_LB, b * s, LANES),
                       b_out.reshape(b * s, ATTN_WIDTH), w_out.astype(BF16),
                       conv_ln_g, conv_ln_b, norm2_g)
    out = _conv_ffn(x1, h2, w_up.astype(BF16), ffn_dw_w, ffn_dw_b, w_down.astype(BF16), s)
    return out.reshape(b, s, d)
```

```python
import functools

import jax
import jax.numpy as jnp
from jax import lax
from jax.experimental import pallas as pl
from jax.experimental.pallas import tpu as pltpu

D_MODEL = 2048
CONV_CH = 1024
ATTN_WIDTH = 1024
HEAD_DIM = 128
N_HEADS = ATTN_WIDTH // HEAD_DIM
CONV_WIDTH = 31
CONV_HALF = (CONV_WIDTH - 1) // 2
FFN_DIM = 5632
RMS_EPS = 1e-6
LN_EPS = 1e-5
NEG_BIG = -1e30
BAND_R = 64
DILATIONS = (1, 4, 16)
LOG2E = 1.4426950408889634

LANES = 128
BF16_ROWS = 16
VMEM_LIMIT = 56 * 1024 * 1024

F32 = jnp.float32
BF16 = jnp.bfloat16


def _cparams(sem):
    return pltpu.CompilerParams(dimension_semantics=sem, vmem_limit_bytes=VMEM_LIMIT)


IN_TM = 1024
IN_TN = 1024
assert CONV_CH == IN_TN and ATTN_WIDTH == IN_TN
STEP_VAL, STEP_GATE, STEP_Q, STEP_K, STEP_V = range(5)
SLABS_PER_STEP = IN_TN // LANES
U_SLABS = 4 * SLABS_PER_STEP
Q_SLAB0 = SLABS_PER_STEP


def _in_proj_kernel(x_ref, g_ref, w_ref, qg_ref, kg_ref, o_ref, h_scr):
    j = pl.program_id(1)

    @pl.when(j == 0)
    def _():
        xf = x_ref[...]
        ms = jnp.mean(xf * xf, axis=-1, keepdims=True)
        h_scr[...] = (xf * lax.rsqrt(ms + RMS_EPS) * g_ref[...]).astype(BF16)

    def proj():
        return jnp.dot(h_scr[...], w_ref[...], preferred_element_type=F32)

    def store_slabs(res, gain=None):
        for sb in range(SLABS_PER_STEP):
            t = res[:, sb * LANES:(sb + 1) * LANES]
            if gain is not None:
                ms = jnp.mean(t * t, axis=-1, keepdims=True)
                t = t * lax.rsqrt(ms + RMS_EPS) * gain
            o_ref[sb] = t

    @pl.when((j == STEP_VAL) | (j == STEP_V))
    def _():
        store_slabs(proj())

    @pl.when(j == STEP_GATE)
    def _():
        gate = jax.nn.sigmoid(proj())
        for sb in range(SLABS_PER_STEP):
            o_ref[sb] = o_ref[sb] * gate[:, sb * LANES:(sb + 1) * LANES]

    @pl.when(j == STEP_Q)
    def _():
        store_slabs(proj(), qg_ref[...] * (HEAD_DIM ** -0.5 * LOG2E))

    @pl.when(j == STEP_K)
    def _():
        store_slabs(proj(), kg_ref[...])


def _in_proj(x2, g, w_bf, q_norm_g, k_norm_g):
    m, d = x2.shape
    tm, tn = IN_TM, IN_TN
    gain = pl.BlockSpec((1, HEAD_DIM), lambda i, j: (0, 0))
    return pl.pallas_call(
        _in_proj_kernel,
        grid=(m // tm, w_bf.shape[1] // tn),
        in_specs=[
            pl.BlockSpec((tm, d), lambda i, j: (i, 0)),
            pl.BlockSpec((1, d), lambda i, j: (0, 0)),
            pl.BlockSpec((d, tn), lambda i, j: (0, j)),
            gain, gain,
        ],
        out_specs=pl.BlockSpec((SLABS_PER_STEP, tm, LANES),
                               lambda i, j: (jnp.maximum(j - 1, 0), i, 0)),
        out_shape=jax.ShapeDtypeStruct((U_SLABS, m, LANES), F32),
        scratch_shapes=[pltpu.VMEM((tm, d), BF16)],
        compiler_params=_cparams(("parallel", "arbitrary")),
        name="in_proj",
    )(x2, g.reshape(1, d), w_bf, q_norm_g.reshape(1, HEAD_DIM), k_norm_g.reshape(1, HEAD_DIM))


CONV_TT = 256
CONV_HALO = 16
CONV_RC = 64
CONV_LB = CONV_CH // LANES


def _conv_kernel(a_ref, pa_ref, na_ref, w_ref, b_ref, o_ref, a_scr):
    ti = pl.program_id(1)
    nt = pl.num_programs(1)
    tt = CONV_TT

    keep_prev = (ti > 0).astype(F32)
    keep_next = (ti < nt - 1).astype(F32)
    for lb in range(CONV_LB):
        a_scr[lb, 0:CONV_HALO, :] = pa_ref[lb, 0] * keep_prev
        a_scr[lb, CONV_HALO:CONV_HALO + tt, :] = a_ref[lb, 0]
        a_scr[lb, CONV_HALO + tt:, :] = na_ref[lb, 0] * keep_next

    def lane_block(lb, carry):
        for rc in range(tt // CONV_RC):
            r0 = rc * CONV_RC + CONV_HALO - CONV_HALF
            acc = jnp.broadcast_to(b_ref[lb], (CONV_RC, LANES))
            for k in range(CONV_WIDTH):
                acc = acc + a_scr[lb, r0 + k:r0 + k + CONV_RC, :] * w_ref[lb, k:k + 1, :]
            o_ref[lb, 0, rc * CONV_RC:(rc + 1) * CONV_RC, :] = acc
        return carry

    lax.fori_loop(0, CONV_LB, lane_block, 0)


def _conv_group(u4, conv_dw_w, conv_dw_b):
    _, b, s, _ = u4.shape
    tt, halo = CONV_TT, CONV_HALO
    nt = s // tt
    hb = tt // halo
    n_hblk = s // halo
    w3 = conv_dw_w.reshape(CONV_WIDTH, CONV_LB, LANES).transpose(1, 0, 2)
    b3 = conv_dw_b.reshape(CONV_LB, 1, LANES)
    full = lambda shape: pl.BlockSpec(shape, lambda bi, ti: (0,) * len(shape))
    return pl.pallas_call(
        _conv_kernel,
        grid=(b, nt),
        in_specs=[
            pl.BlockSpec((CONV_LB, 1, tt, LANES), lambda bi, ti: (0, bi, ti, 0)),
            pl.BlockSpec((CONV_LB, 1, halo, LANES),
                         lambda bi, ti: (0, bi, jnp.maximum(ti * hb - 1, 0), 0)),
            pl.BlockSpec((CONV_LB, 1, halo, LANES),
                         lambda bi, ti: (0, bi, jnp.minimum((ti + 1) * hb, n_hblk - 1), 0)),
            full((CONV_LB, CONV_WIDTH, LANES)), full((CONV_LB, 1, LANES))],
        out_specs=pl.BlockSpec((CONV_LB, 1, tt, LANES), lambda bi, ti: (0, bi, ti, 0)),
        out_shape=jax.ShapeDtypeStruct((CONV_LB, b, s, LANES), F32),
        scratch_shapes=[pltpu.VMEM((CONV_LB, tt + 2 * halo, LANES), F32)],
        compiler_params=_cparams(("parallel", "arbitrary")),
        name="conv_group",
    )(u4, u4, u4, w3, b3)


ATT_QB = 128
ATT_KB = ATT_QB + 2 * BAND_R
ATT_C4 = 4


def _attn_kernel(slope_ref, q_ref, k_ref, v_ref, o_ref,
                 q4, k4, v4, bias_scr, acc_scr, m_scr, l_scr, out_scr):
    h = pl.program_id(1)
    s_len = q_ref.shape[0]
    cl = s_len // ATT_C4
    slope = slope_ref[h] * LOG2E

    for c4 in range(ATT_C4):
        dst = pl.ds(c4 * cl, cl)
        src = pl.ds(c4, cl, stride=ATT_C4)
        q4[dst, :] = q_ref[src, :]
        k4[dst, :] = k_ref[src, :]
        v4[dst, :] = v_ref[src, :]

    rows = lax.broadcasted_iota(jnp.int32, (ATT_QB, ATT_KB), 0)
    cols = lax.broadcasted_iota(jnp.int32, (ATT_QB, ATT_KB), 1)
    for w, dil in enumerate(DILATIONS):
        for e in range(3):
            off = jnp.abs(cols - rows - e * BAND_R)
            bias = jnp.where(off <= BAND_R, -(slope * dil) * off.astype(F32), NEG_BIG)
            bias_scr[3 * w + e] = bias

    ones_rhs = jnp.ones((ATT_KB, LANES), BF16)

    def block(w, qb, kb, vb, out_idx, bias):
        qb, kb, vb = qb.astype(BF16), kb.astype(BF16), vb.astype(BF16)
        sc = lax.dot_general(qb, kb, (((1,), (1,)), ((), ())), preferred_element_type=F32)
        sc = sc + bias
        m = jnp.max(sc, axis=-1, keepdims=True)
        p = jnp.exp2(sc - m).astype(BF16)
        acc = jnp.dot(p, jnp.concatenate([vb, ones_rhs[:vb.shape[0]]], axis=1),
                      preferred_element_type=F32)
        acc_scr[w, out_idx, :] = acc[:, :HEAD_DIM]
        l_scr[w, out_idx, :] = acc[:, HEAD_DIM:]
        m_scr[w, out_idx, :] = jnp.broadcast_to(m, (ATT_QB, LANES))

    def key_start(q0, class_len, nk):
        return min(max(q0 - BAND_R, 0), class_len - nk)

    for i in range(s_len // ATT_QB):
        q0 = i * ATT_QB
        k0 = key_start(q0, s_len, ATT_KB)
        qi, ki = pl.ds(q0, ATT_QB), pl.ds(k0, ATT_KB)
        block(0, q_ref[qi, :], k_ref[ki, :], v_ref[ki, :], qi,
              bias_scr[(q0 - k0) // BAND_R])

    for c4 in range(ATT_C4):
        for i in range(cl // ATT_QB):
            q0 = i * ATT_QB
            k0 = key_start(q0, cl, ATT_KB)
            qi, ki = pl.ds(c4 * cl + q0, ATT_QB), pl.ds(c4 * cl + k0, ATT_KB)
            block(1, q4[qi, :], k4[ki, :], v4[ki, :], qi, bias_scr[3 + (q0 - k0) // BAND_R])

    sub_len = s_len // DILATIONS[2]
    for c4 in range(ATT_C4):
        for c in range(DILATIONS[2] // ATT_C4):
            idx = pl.ds(c4 * cl + c, sub_len, stride=DILATIONS[2] // ATT_C4)
            block(2, q4[idx, :], k4[idx, :], v4[idx, :], idx, bias_scr[6][:, :sub_len])

    chunk = 256
    for c4 in range(ATT_C4):
        for r in range(cl // chunk):
            nat = pl.ds(c4 + ATT_C4 * r * chunk, chunk, stride=ATT_C4)
            grp = pl.ds(c4 * cl + r * chunk, chunk)
            m0, m1, m2 = m_scr[0, nat, :], m_scr[1, grp, :], m_scr[2, grp, :]
            mm = jnp.maximum(jnp.maximum(m0, m1), m2)
            a0, a1, a2 = jnp.exp2(m0 - mm), jnp.exp2(m1 - mm), jnp.exp2(m2 - mm)
            num = a0 * acc_scr[0, nat, :] + a1 * acc_scr[1, grp, :] + a2 * acc_scr[2, grp, :]
            den = a0 * l_scr[0, nat, :] + a1 * l_scr[1, grp, :] + a2 * l_scr[2, grp, :]
            out_scr[nat, :] = num / den
    o_ref[0] = out_scr[...].astype(o_ref.dtype)


def _attn_group(u4):
    _, b, s, _ = u4.shape
    slopes = jnp.asarray([2.0 ** (-8.0 * (i + 1) / N_HEADS) for i in range(N_HEADS)], F32)
    head = lambda base: pl.BlockSpec((None, None, s, HEAD_DIM),
                                     lambda bi, hi: (base + hi, bi, 0, 0))
    return pl.pallas_call(
        _attn_kernel,
        grid=(b, N_HEADS),
        in_specs=[pl.BlockSpec(memory_space=pltpu.SMEM),
                  head(Q_SLAB0), head(Q_SLAB0 + N_HEADS), head(Q_SLAB0 + 2 * N_HEADS)],
        out_specs=pl.BlockSpec((1, s, HEAD_DIM), lambda bi, hi: (bi, 0, hi)),
        out_shape=jax.ShapeDtypeStruct((b, s, ATTN_WIDTH), BF16),
        scratch_shapes=[pltpu.VMEM((s, HEAD_DIM), F32),
                        pltpu.VMEM((s, HEAD_DIM), F32),
                        pltpu.VMEM((s, HEAD_DIM), F32),
                        pltpu.VMEM((9, ATT_QB, ATT_KB), F32),
                        pltpu.VMEM((3, s, HEAD_DIM), F32),
                        pltpu.VMEM((3, s, LANES), F32),
                        pltpu.VMEM((3, s, LANES), F32),
                        pltpu.VMEM((s, HEAD_DIM), F32)],
        compiler_params=_cparams(("parallel", "arbitrary")),
        name="attn_group",
    )(slopes, u4, u4, u4)


def _out_proj_kernel(x_ref, y_ref, b_ref, w_ref, lg_ref, lb_ref, g_ref, o_ref, h_ref):
    acc = jnp.dot(b_ref[...], w_ref[CONV_CH:, :], preferred_element_type=F32)

    ys = [y_ref[lb] for lb in range(CONV_LB)]
    tot = ys[0]
    for t in ys[1:]:
        tot = tot + t
    mu = jnp.sum(tot, axis=-1, keepdims=True) * (1.0 / CONV_CH)
    sq = None
    for t in ys:
        c = t - mu
        sq = c * c if sq is None else sq + c * c
    var = jnp.sum(sq, axis=-1, keepdims=True) * (1.0 / CONV_CH)
    rstd = lax.rsqrt(var + LN_EPS)
    a_cols = []
    for lb, t in enumerate(ys):
        ls = slice(lb * LANES, (lb + 1) * LANES)
        z = (t - mu) * rstd * lg_ref[:, ls] + lb_ref[:, ls]
        a_cols.append((z * jax.nn.sigmoid(z)).astype(BF16))
    a = jnp.concatenate(a_cols, axis=1)

    acc = acc + jnp.dot(a, w_ref[0:CONV_CH, :], preferred_element_type=F32)
    x1 = x_ref[...] + acc
    o_ref[...] = x1
    ms = jnp.mean(x1 * x1, axis=-1, keepdims=True)
    h_ref[...] = (x1 * lax.rsqrt(ms + RMS_EPS) * g_ref[...]).astype(BF16)


def _out_proj(x2, y_slabs, b2, w_bf, conv_ln_g, conv_ln_b, norm2_g, tm=512):
    m, d = x2.shape
    row = lambda shape: pl.BlockSpec(shape, lambda i: (0,) * len(shape))
    return pl.pallas_call(
        _out_proj_kernel,
        grid=(m // tm,),
        in_specs=[
            pl.BlockSpec((tm, d), lambda i: (i, 0)),
            pl.BlockSpec((CONV_LB, tm, LANES), lambda i: (0, i, 0)),
            pl.BlockSpec((tm, ATTN_WIDTH), lambda i: (i, 0)),
            row((d, d)), row((1, CONV_CH)), row((1, CONV_CH)), row((1, d)),
        ],
        out_specs=[pl.BlockSpec((tm, d), lambda i: (i, 0)),
                   pl.BlockSpec((tm, d), lambda i: (i, 0))],
        out_shape=[jax.ShapeDtypeStruct((m, d), F32),
                   jax.ShapeDtypeStruct((m, d), BF16)],
        compiler_params=_cparams(("parallel",)),
        name="out_proj",
    )(x2, y_slabs, b2, w_bf, conv_ln_g.reshape(1, CONV_CH), conv_ln_b.reshape(1, CONV_CH),
      norm2_g.reshape(1, d))


FFN_TM = 1024
FFN_TF = 512
FFN_PAD = BF16_ROWS
FFN_XC = 256
FFN_NXC = D_MODEL // FFN_XC


def _ffn_kernel(h_ref, hp_ref, hn_ref, x_ref, wg_ref, wv_ref, cwg_ref, cwv_ref,
                cbg_ref, cbv_ref, wd_ref, o_ref, h_scr, ug_scr, uv_scr, *, tiles_per_seq):
    i = pl.program_id(0)
    j = pl.program_id(1)
    tm = FFN_TM

    @pl.when(j == 0)
    def _():
        keep_prev = ((i % tiles_per_seq) != 0).astype(F32)
        keep_next = ((i % tiles_per_seq) != tiles_per_seq - 1).astype(F32)
        rid = lax.broadcasted_iota(jnp.int32, (FFN_PAD, 1), 0)
        hp = hp_ref[...].astype(F32)[FFN_PAD - 1:FFN_PAD, :] * keep_prev
        hn = hn_ref[...].astype(F32)[0:1, :] * keep_next
        h_scr[0:FFN_PAD, :] = jnp.where(rid == FFN_PAD - 1, hp, 0.0).astype(BF16)
        h_scr[FFN_PAD:FFN_PAD + tm, :] = h_ref[...]
        h_scr[FFN_PAD + tm:, :] = jnp.where(rid == 0, hn, 0.0).astype(BF16)
        o_ref[...] = jnp.zeros_like(o_ref)

    for c in range(FFN_NXC):
        @pl.when(j == c)
        def _(c=c):
            o_ref[:, c * FFN_XC:(c + 1) * FFN_XC] += x_ref[...]

    hh = h_scr[...]

    def up_conv(w_ref, cw_ref, cb_ref, u_scr):
        u = jnp.dot(hh, w_ref[...], preferred_element_type=F32)
        cols = []
        for c in range(FFN_TF // LANES):
            ls = slice(c * LANES, (c + 1) * LANES)
            u_scr[c] = u[:, ls]
            cols.append(cw_ref[0:1, ls] * u_scr[c, FFN_PAD - 1:FFN_PAD - 1 + tm, :]
                        + cw_ref[1:2, ls] * u_scr[c, FFN_PAD:FFN_PAD + tm, :]
                        + cw_ref[2:3, ls] * u_scr[c, FFN_PAD + 1:FFN_PAD + 1 + tm, :]
                        + cb_ref[:, ls])
        return jnp.concatenate(cols, axis=1)

    gte = up_conv(wg_ref, cwg_ref, cbg_ref, ug_scr)
    val = up_conv(wv_ref, cwv_ref, cbv_ref, uv_scr)
    act = (gte * jax.nn.sigmoid(gte) * val).astype(BF16)
    o_ref[...] += jnp.dot(act, wd_ref[...], preferred_element_type=F32)


def _conv_ffn(x1, h2, w_up_bf, ffn_dw_w, ffn_dw_b, w_down_bf, seq_len):
    m, d = x1.shape
    tm, tf = FFN_TM, FFN_TF
    nf = FFN_DIM // tf
    assert nf >= FFN_NXC
    hb = tm // FFN_PAD
    n_hblk = m // FFN_PAD
    kern = functools.partial(_ffn_kernel, tiles_per_seq=seq_len // tm)
    return pl.pallas_call(
        kern,
        grid=(m // tm, nf),
        in_specs=[
            pl.BlockSpec((tm, d), lambda i, j: (i, 0)),
            pl.BlockSpec((FFN_PAD, d), lambda i, j: (jnp.maximum(i * hb - 1, 0), 0)),
            pl.BlockSpec((FFN_PAD, d), lambda i, j: (jnp.minimum((i + 1) * hb, n_hblk - 1), 0)),
            pl.BlockSpec((tm, FFN_XC), lambda i, j: (i, jnp.minimum(j, FFN_NXC - 1))),
            pl.BlockSpec((d, tf), lambda i, j: (0, j)),
            pl.BlockSpec((d, tf), lambda i, j: (0, nf + j)),
            pl.BlockSpec((3, tf), lambda i, j: (0, j)),
            pl.BlockSpec((3, tf), lambda i, j: (0, nf + j)),
            pl.BlockSpec((1, tf), lambda i, j: (0, j)),
            pl.BlockSpec((1, tf), lambda i, j: (0, nf + j)),
            pl.BlockSpec((tf, d), lambda i, j: (j, 0)),
        ],
        out_specs=pl.BlockSpec((tm, d), lambda i, j: (i, 0)),
        out_shape=jax.ShapeDtypeStruct((m, d), F32),
        scratch_shapes=[pltpu.VMEM((tm + 2 * FFN_PAD, d), BF16),
                        pltpu.VMEM((tf // LANES, tm + 2 * FFN_PAD, LANES), F32),
                        pltpu.VMEM((tf // LANES, tm + 2 * FFN_PAD, LANES), F32)],
        compiler_params=_cparams(("parallel", "arbitrary")),
        name="conv_ffn",
    )(h2, h2, h2, x1, w_up_bf, w_up_bf, ffn_dw_w, ffn_dw_w,
      ffn_dw_b.reshape(1, -1), ffn_dw_b.reshape(1, -1), w_down_bf)


def kernel(x, norm1_g, w_in, conv_dw_w, conv_dw_b, conv_ln_g, conv_ln_b, q_norm_g, k_norm_g,
           w_out, norm2_g, w_up, ffn_dw_w, ffn_dw_b, w_down):
    b, s, d = x.shape
    x2 = x.reshape(b * s, d)
    u = _in_proj(x2, norm1_g, w_in.astype(BF16), q_norm_g, k_norm_g)
    u4 = u.reshape(U_SLABS, b, s, LANES)
    y_slabs = _conv_group(u4, conv_dw_w, conv_dw_b)
    b_out = _attn_group(u4)
    x1, h2 = _out_proj(x2, y_slabs.reshape(CONV_LB, b * s, LANES),
                       b_out.reshape(b * s, ATTN_WIDTH), w_out.astype(BF16),
                       conv_ln_g, conv_ln_b, norm2_g)
    out = _conv_ffn(x1, h2, w_up.astype(BF16), ffn_dw_w, ffn_dw_b, w_down.astype(BF16), s)
    return out.reshape(b, s, d)
```

```python
import functools

import jax
import jax.numpy as jnp
from jax import lax
from jax.experimental import pallas as pl
from jax.experimental.pallas import tpu as pltpu

D_MODEL = 2048
CONV_CH = 1024
ATTN_WIDTH = 1024
HEAD_DIM = 128
N_HEADS = ATTN_WIDTH // HEAD_DIM
CONV_WIDTH = 31
CONV_HALF = (CONV_WIDTH - 1) // 2
FFN_DIM = 5632
RMS_EPS = 1e-6
LN_EPS = 1e-5
NEG_BIG = -1e30
BAND_R = 64
DILATIONS = (1, 4, 16)
LOG2E = 1.4426950408889634

LANES = 128
BF16_ROWS = 16
VMEM_LIMIT = 56 * 1024 * 1024

F32 = jnp.float32
BF16 = jnp.bfloat16


def _cparams(sem):
    return pltpu.CompilerParams(dimension_semantics=sem, vmem_limit_bytes=VMEM_LIMIT)


IN_TM = 1024
IN_TN = 1024
IN_XPARTS = 4
assert CONV_CH == IN_TN and ATTN_WIDTH == IN_TN
STEP_VAL, STEP_GATE, STEP_Q, STEP_K, STEP_V = range(5)
SLABS_PER_STEP = IN_TN // LANES
U_SLABS = 4 * SLABS_PER_STEP
Q_SLAB0 = SLABS_PER_STEP


def _in_proj_kernel(*refs):
    x_parts = refs[:IN_XPARTS]
    g_ref, w_ref, qg_ref, kg_ref, o_ref, h_scr = refs[IN_XPARTS:]
    j = pl.program_id(1)
    rows = IN_TM // IN_XPARTS

    @pl.when(j == 0)
    def _():
        for p, x_ref in enumerate(x_parts):
            xf = x_ref[...]
            ms = jnp.mean(xf * xf, axis=-1, keepdims=True)
            h_scr[p * rows:(p + 1) * rows, :] = (
                xf * lax.rsqrt(ms + RMS_EPS) * g_ref[...]).astype(BF16)

    def proj():
        return jnp.dot(h_scr[...], w_ref[...], preferred_element_type=F32)

    def store_slabs(res, gain=None):
        for sb in range(SLABS_PER_STEP):
            t = res[:, sb * LANES:(sb + 1) * LANES]
            if gain is not None:
                ms = jnp.mean(t * t, axis=-1, keepdims=True)
                t = t * lax.rsqrt(ms + RMS_EPS) * gain
            o_ref[sb] = t

    @pl.when((j == STEP_VAL) | (j == STEP_V))
    def _():
        store_slabs(proj())

    @pl.when(j == STEP_GATE)
    def _():
        gate = jax.nn.sigmoid(proj())
        for sb in range(SLABS_PER_STEP):
            o_ref[sb] = o_ref[sb] * gate[:, sb * LANES:(sb + 1) * LANES]

    @pl.when(j == STEP_Q)
    def _():
        store_slabs(proj(), qg_ref[...] * (HEAD_DIM ** -0.5 * LOG2E))

    @pl.when(j == STEP_K)
    def _():
        store_slabs(proj(), kg_ref[...])


def _in_proj(x2, g, w_bf, q_norm_g, k_norm_g):
    m, d = x2.shape
    tm, tn = IN_TM, IN_TN
    gain = pl.BlockSpec((1, HEAD_DIM), lambda i, j: (0, 0))
    n_tiles = m // tm
    n_steps = w_bf.shape[1] // tn
    assert n_steps > IN_XPARTS

    def x_part(p):
        def index(i, j):
            nxt = (j + n_steps - 2 - p) // (n_steps - 1)
            return (IN_XPARTS * jnp.minimum(i + nxt, n_tiles - 1) + p, 0)
        return pl.BlockSpec((tm // IN_XPARTS, d), index)

    return pl.pallas_call(
        _in_proj_kernel,
        grid=(n_tiles, n_steps),
        in_specs=[x_part(p) for p in range(IN_XPARTS)] + [
            pl.BlockSpec((1, d), lambda i, j: (0, 0)),
            pl.BlockSpec((d, tn), lambda i, j: (0, j)),
            gain, gain,
        ],
        out_specs=pl.BlockSpec((SLABS_PER_STEP, tm, LANES),
                               lambda i, j: (jnp.maximum(j - 1, 0), i, 0)),
        out_shape=jax.ShapeDtypeStruct((U_SLABS, m, LANES), F32),
        scratch_shapes=[pltpu.VMEM((tm, d), BF16)],
        compiler_params=_cparams(("parallel", "arbitrary")),
        name="in_proj",
    )(*([x2] * IN_XPARTS), g.reshape(1, d), w_bf,
      q_norm_g.reshape(1, HEAD_DIM), k_norm_g.reshape(1, HEAD_DIM))


CONV_TT = 256
CONV_HALO = 16
CONV_RC = 64
CONV_LB = CONV_CH // LANES


def _cast_spec(w, n_steps, index):
    rows = w.size // (n_steps * LANES)
    assert rows * n_steps * LANES == w.size and rows % BF16_ROWS == 0
    return w.reshape(n_steps, rows, LANES), pl.BlockSpec((1, rows, LANES), index)


def _conv_kernel(a_ref, pa_ref, na_ref, w_ref, b_ref, wf_ref, o_ref, wb_ref, a_scr):
    ti = pl.program_id(1)
    nt = pl.num_programs(1)
    tt = CONV_TT

    wb_ref[...] = wf_ref[...].astype(BF16)

    keep_prev = (ti > 0).astype(F32)
    keep_next = (ti < nt - 1).astype(F32)
    for lb in range(CONV_LB):
        a_scr[lb, 0:CONV_HALO, :] = pa_ref[lb, 0] * keep_prev
        a_scr[lb, CONV_HALO:CONV_HALO + tt, :] = a_ref[lb, 0]
        a_scr[lb, CONV_HALO + tt:, :] = na_ref[lb, 0] * keep_next

    def lane_block(lb, carry):
        for rc in range(tt // CONV_RC):
            r0 = rc * CONV_RC + CONV_HALO - CONV_HALF
            acc = jnp.broadcast_to(b_ref[lb], (CONV_RC, LANES))
            for k in range(CONV_WIDTH):
                acc = acc + a_scr[lb, r0 + k:r0 + k + CONV_RC, :] * w_ref[lb, k:k + 1, :]
            o_ref[lb, 0, rc * CONV_RC:(rc + 1) * CONV_RC, :] = acc
        return carry

    lax.fori_loop(0, CONV_LB, lane_block, 0)


def _conv_group(u4, conv_dw_w, conv_dw_b, w_cast):
    _, b, s, _ = u4.shape
    tt, halo = CONV_TT, CONV_HALO
    nt = s // tt
    hb = tt // halo
    n_hblk = s // halo
    w3 = conv_dw_w.reshape(CONV_WIDTH, CONV_LB, LANES).transpose(1, 0, 2)
    b3 = conv_dw_b.reshape(CONV_LB, 1, LANES)
    full = lambda shape: pl.BlockSpec(shape, lambda bi, ti: (0,) * len(shape))
    wf, cast_spec = _cast_spec(w_cast, b * nt, lambda bi, ti: (bi * nt + ti, 0, 0))
    y, wb = pl.pallas_call(
        _conv_kernel,
        grid=(b, nt),
        in_specs=[
            pl.BlockSpec((CONV_LB, 1, tt, LANES), lambda bi, ti: (0, bi, ti, 0)),
            pl.BlockSpec((CONV_LB, 1, halo, LANES),
                         lambda bi, ti: (0, bi, jnp.maximum(ti * hb - 1, 0), 0)),
            pl.BlockSpec((CONV_LB, 1, halo, LANES),
                         lambda bi, ti: (0, bi, jnp.minimum((ti + 1) * hb, n_hblk - 1), 0)),
            full((CONV_LB, CONV_WIDTH, LANES)), full((CONV_LB, 1, LANES)), cast_spec],
        out_specs=[pl.BlockSpec((CONV_LB, 1, tt, LANES), lambda bi, ti: (0, bi, ti, 0)),
                   cast_spec],
        out_shape=[jax.ShapeDtypeStruct((CONV_LB, b, s, LANES), F32),
                   jax.ShapeDtypeStruct(wf.shape, BF16)],
        scratch_shapes=[pltpu.VMEM((CONV_LB, tt + 2 * halo, LANES), F32)],
        compiler_params=_cparams(("parallel", "arbitrary")),
        name="conv_group",
    )(u4, u4, u4, w3, b3, wf)
    return y, wb.reshape(w_cast.shape)


ATT_QB = 128
ATT_KB = 256
ATT_NE = (ATT_KB - ATT_QB) // BAND_R + 1
ATT_C4 = 4


def _attn_kernel(slope_ref, q_ref, k_ref, v_ref, wf_ref, o_ref, wb_ref,
                 q4, k4, v4, bias_scr, acc_scr, m_scr, l_scr, out_scr):
    h = pl.program_id(1)
    s_len = q_ref.shape[0]
    cl = s_len // ATT_C4
    slope = slope_ref[h] * LOG2E

    wb_ref[...] = wf_ref[...].astype(BF16)

    for c4 in range(ATT_C4):
        dst = pl.ds(c4 * cl, cl)
        src = pl.ds(c4, cl, stride=ATT_C4)
        q4[dst, :] = q_ref[src, :]
        k4[dst, :] = k_ref[src, :]
        v4[dst, :] = v_ref[src, :]

    rows = lax.broadcasted_iota(jnp.int32, (ATT_QB, ATT_KB), 0)
    cols = lax.broadcasted_iota(jnp.int32, (ATT_QB, ATT_KB), 1)
    for w, dil in enumerate(DILATIONS):
        for e in range(ATT_NE):
            off = jnp.abs(cols - rows - e * BAND_R)
            bias = jnp.where(off <= BAND_R, -(slope * dil) * off.astype(F32), NEG_BIG)
            bias_scr[ATT_NE * w + e] = bias

    ones_rhs = jnp.ones((ATT_KB, LANES), BF16)

    def block(w, qb, kb, vb, out_idx, bias):
        qb, kb, vb = qb.astype(BF16), kb.astype(BF16), vb.astype(BF16)
        sc = lax.dot_general(qb, kb, (((1,), (1,)), ((), ())), preferred_element_type=F32)
        sc = sc + bias
        m = jnp.max(sc, axis=-1, keepdims=True)
        p = jnp.exp2(sc - m).astype(BF16)
        acc = jnp.dot(p, jnp.concatenate([vb, ones_rhs[:vb.shape[0]]], axis=1),
                      preferred_element_type=F32)
        acc_scr[w, out_idx, :] = acc[:, :HEAD_DIM]
        l_scr[w, out_idx, :] = acc[:, HEAD_DIM:]
        m_scr[w, out_idx, :] = jnp.broadcast_to(m, (ATT_QB, LANES))

    def key_start(q0, class_len, nk):
        return min(max(q0 - BAND_R, 0), class_len - nk)

    for i in range(s_len // ATT_QB):
        q0 = i * ATT_QB
        k0 = key_start(q0, s_len, ATT_KB)
        qi, ki = pl.ds(q0, ATT_QB), pl.ds(k0, ATT_KB)
        block(0, q_ref[qi, :], k_ref[ki, :], v_ref[ki, :], qi,
              bias_scr[(q0 - k0) // BAND_R])

    for c4 in range(ATT_C4):
        for i in range(cl // ATT_QB):
            q0 = i * ATT_QB
            k0 = key_start(q0, cl, ATT_KB)
            qi, ki = pl.ds(c4 * cl + q0, ATT_QB), pl.ds(c4 * cl + k0, ATT_KB)
            block(1, q4[qi, :], k4[ki, :], v4[ki, :], qi,
                  bias_scr[ATT_NE + (q0 - k0) // BAND_R])

    sub_len = s_len // DILATIONS[2]
    step4 = DILATIONS[2] // ATT_C4
    for c4 in range(ATT_C4):
        for c in range(step4):
            ki = pl.ds(c4 * cl + c, sub_len, stride=step4)
            for q0 in range(0, sub_len, ATT_QB):
                qi = pl.ds(c4 * cl + c + step4 * q0, ATT_QB, stride=step4)
                block(2, q4[qi, :], k4[ki, :], v4[ki, :], qi,
                      bias_scr[2 * ATT_NE + q0 // BAND_R][:, :sub_len])

    chunk = 256
    for c4 in range(ATT_C4):
        for r in range(cl // chunk):
            nat = pl.ds(c4 + ATT_C4 * r * chunk, chunk, stride=ATT_C4)
            grp = pl.ds(c4 * cl + r * chunk, chunk)
            m0, m1, m2 = m_scr[0, nat, :], m_scr[1, grp, :], m_scr[2, grp, :]
            mm = jnp.maximum(jnp.maximum(m0, m1), m2)
            a0, a1, a2 = jnp.exp2(m0 - mm), jnp.exp2(m1 - mm), jnp.exp2(m2 - mm)
            num = a0 * acc_scr[0, nat, :] + a1 * acc_scr[1, grp, :] + a2 * acc_scr[2, grp, :]
            den = a0 * l_scr[0, nat, :] + a1 * l_scr[1, grp, :] + a2 * l_scr[2, grp, :]
            out_scr[nat, :] = num / den
    o_ref[0] = out_scr[...].astype(o_ref.dtype)


def _attn_group(u4, w_cast):
    _, b, s, _ = u4.shape
    slopes = jnp.asarray([2.0 ** (-8.0 * (i + 1) / N_HEADS) for i in range(N_HEADS)], F32)
    head = lambda base: pl.BlockSpec((None, None, s, HEAD_DIM),
                                     lambda bi, hi: (base + hi, bi, 0, 0))
    wf, cast_spec = _cast_spec(w_cast, b * N_HEADS, lambda bi, hi: (bi * N_HEADS + hi, 0, 0))
    bm, wb = pl.pallas_call(
        _attn_kernel,
        grid=(b, N_HEADS),
        in_specs=[pl.BlockSpec(memory_space=pltpu.SMEM),
                  head(Q_SLAB0), head(Q_SLAB0 + N_HEADS), head(Q_SLAB0 + 2 * N_HEADS),
                  cast_spec],
        out_specs=[pl.BlockSpec((1, s, HEAD_DIM), lambda bi, hi: (bi, 0, hi)), cast_spec],
        out_shape=[jax.ShapeDtypeStruct((b, s, ATTN_WIDTH), BF16),
                   jax.ShapeDtypeStruct(wf.shape, BF16)],
        scratch_shapes=[pltpu.VMEM((s, HEAD_DIM), F32),
                        pltpu.VMEM((s, HEAD_DIM), F32),
                        pltpu.VMEM((s, HEAD_DIM), F32),
                        pltpu.VMEM((len(DILATIONS) * ATT_NE, ATT_QB, ATT_KB), F32),
                        pltpu.VMEM((3, s, HEAD_DIM), F32),
                        pltpu.VMEM((3, s, LANES), F32),
                        pltpu.VMEM((3, s, LANES), F32),
                        pltpu.VMEM((s, HEAD_DIM), F32)],
        compiler_params=_cparams(("parallel", "arbitrary")),
        name="attn_group",
    )(slopes, u4, u4, u4, wf)
    return bm, wb.reshape(w_cast.shape)


def _out_proj_kernel(x_ref, y_ref, b_ref, w_ref, lg_ref, lb_ref, g_ref, o_ref, h_ref):
    acc = jnp.dot(b_ref[...], w_ref[CONV_CH:, :], preferred_element_type=F32)

    ys = [y_ref[lb] for lb in range(CONV_LB)]
    tot = ys[0]
    for t in ys[1:]:
        tot = tot + t
    mu = jnp.sum(tot, axis=-1, keepdims=True) * (1.0 / CONV_CH)
    sq = None
    for t in ys:
        c = t - mu
        sq = c * c if sq is None else sq + c * c
    var = jnp.sum(sq, axis=-1, keepdims=True) * (1.0 / CONV_CH)
    rstd = lax.rsqrt(var + LN_EPS)
    a_cols = []
    for lb, t in enumerate(ys):
        ls = slice(lb * LANES, (lb + 1) * LANES)
        z = (t - mu) * rstd * lg_ref[:, ls] + lb_ref[:, ls]
        a_cols.append((z * jax.nn.sigmoid(z)).astype(BF16))
    a = jnp.concatenate(a_cols, axis=1)

    acc = acc + jnp.dot(a, w_ref[0:CONV_CH, :], preferred_element_type=F32)
    x1 = x_ref[...] + acc
    o_ref[...] = x1
    ms = jnp.mean(x1 * x1, axis=-1, keepdims=True)
    h_ref[...] = (x1 * lax.rsqrt(ms + RMS_EPS) * g_ref[...]).astype(BF16)


def _out_proj(x2, y_slabs, b2, w_bf, conv_ln_g, conv_ln_b, norm2_g, tm=512):
    m, d = x2.shape
    row = lambda shape: pl.BlockSpec(shape, lambda i: (0,) * len(shape))
    return pl.pallas_call(
        _out_proj_kernel,
        grid=(m // tm,),
        in_specs=[
            pl.BlockSpec((tm, d), lambda i: (i, 0)),
            pl.BlockSpec((CONV_LB, tm, LANES), lambda i: (0, i, 0)),
            pl.BlockSpec((tm, ATTN_WIDTH), lambda i: (i, 0)),
            row((d, d)), row((1, CONV_CH)), row((1, CONV_CH)), row((1, d)),
        ],
        out_specs=[pl.BlockSpec((tm, d), lambda i: (i, 0)),
                   pl.BlockSpec((tm, d), lambda i: (i, 0))],
        out_shape=[jax.ShapeDtypeStruct((m, d), F32),
                   jax.ShapeDtypeStruct((m, d), BF16)],
        compiler_params=_cparams(("parallel",)),
        name="out_proj",
    )(x2, y_slabs, b2, w_bf, conv_ln_g.reshape(1, CONV_CH), conv_ln_b.reshape(1, CONV_CH),
      norm2_g.reshape(1, d))


FFN_TM = 1024
FFN_TF = 512
FFN_PAD = BF16_ROWS
FFN_XC = 256
FFN_NXC = D_MODEL // FFN_XC


def _ffn_kernel(h_ref, hp_ref, hn_ref, x_ref, wg_ref, wv_ref, cwg_ref, cwv_ref,
                cbg_ref, cbv_ref, wd_ref, o_ref, h_scr, ug_scr, uv_scr, *, tiles_per_seq):
    i = pl.program_id(0)
    j = pl.program_id(1)
    tm = FFN_TM

    @pl.when(j == 0)
    def _():
        keep_prev = ((i % tiles_per_seq) != 0).astype(F32)
        keep_next = ((i % tiles_per_seq) != tiles_per_seq - 1).astype(F32)
        rid = lax.broadcasted_iota(jnp.int32, (FFN_PAD, 1), 0)
        hp = hp_ref[...].astype(F32)[FFN_PAD - 1:FFN_PAD, :] * keep_prev
        hn = hn_ref[...].astype(F32)[0:1, :] * keep_next
        h_scr[0:FFN_PAD, :] = jnp.where(rid == FFN_PAD - 1, hp, 0.0).astype(BF16)
        h_scr[FFN_PAD:FFN_PAD + tm, :] = h_ref[...]
        h_scr[FFN_PAD + tm:, :] = jnp.where(rid == 0, hn, 0.0).astype(BF16)
        o_ref[...] = jnp.zeros_like(o_ref)

    for c in range(FFN_NXC):
        @pl.when(j == c)
        def _(c=c):
            o_ref[:, c * FFN_XC:(c + 1) * FFN_XC] += x_ref[...]

    hh = h_scr[...]

    def up_conv(w_ref, cw_ref, cb_ref, u_scr):
        u = jnp.dot(hh, w_ref[...], preferred_element_type=F32)
        cols = []
        for c in range(FFN_TF // LANES):
            ls = slice(c * LANES, (c + 1) * LANES)
            u_scr[c] = u[:, ls]
            cols.append(cw_ref[0:1, ls] * u_scr[c, FFN_PAD - 1:FFN_PAD - 1 + tm, :]
                        + cw_ref[1:2, ls] * u_scr[c, FFN_PAD:FFN_PAD + tm, :]
                        + cw_ref[2:3, ls] * u_scr[c, FFN_PAD + 1:FFN_PAD + 1 + tm, :]
                        + cb_ref[:, ls])
        return jnp.concatenate(cols, axis=1)

    gte = up_conv(wg_ref, cwg_ref, cbg_ref, ug_scr)
    val = up_conv(wv_ref, cwv_ref, cbv_ref, uv_scr)
    act = (gte * jax.nn.sigmoid(gte) * val).astype(BF16)
    o_ref[...] += jnp.dot(act, wd_ref[...], preferred_element_type=F32)


def _conv_ffn(x1, h2, w_up_bf, ffn_dw_w, ffn_dw_b, w_down_bf, seq_len):
    m, d = x1.shape
    tm, tf = FFN_TM, FFN_TF
    nf = FFN_DIM // tf
    assert nf >= FFN_NXC
    hb = tm // FFN_PAD
    n_hblk = m // FFN_PAD
    kern = functools.partial(_ffn_kernel, tiles_per_seq=seq_len // tm)
    return pl.pallas_call(
        kern,
        grid=(m // tm, nf),
        in_specs=[
            pl.BlockSpec((tm, d), lambda i, j: (i, 0)),
            pl.BlockSpec((FFN_PAD, d), lambda i, j: (jnp.maximum(i * hb - 1, 0), 0)),
            pl.BlockSpec((FFN_PAD, d), lambda i, j: (jnp.minimum((i + 1) * hb, n_hblk - 1), 0)),
            pl.BlockSpec((tm, FFN_XC), lambda i, j: (i, jnp.minimum(j, FFN_NXC - 1))),
            pl.BlockSpec((d, tf), lambda i, j: (0, j)),
            pl.BlockSpec((d, tf), lambda i, j: (0, nf + j)),
            pl.BlockSpec((3, tf), lambda i, j: (0, j)),
            pl.BlockSpec((3, tf), lambda i, j: (0, nf + j)),
            pl.BlockSpec((1, tf), lambda i, j: (0, j)),
            pl.BlockSpec((1, tf), lambda i, j: (0, nf + j)),
            pl.BlockSpec((tf, d), lambda i, j: (j, 0)),
        ],
        out_specs=pl.BlockSpec((tm, d), lambda i, j: (i, 0)),
        out_shape=jax.ShapeDtypeStruct((m, d), F32),
        scratch_shapes=[pltpu.VMEM((tm + 2 * FFN_PAD, d), BF16),
                        pltpu.VMEM((tf // LANES, tm + 2 * FFN_PAD, LANES), F32),
                        pltpu.VMEM((tf // LANES, tm + 2 * FFN_PAD, LANES), F32)],
        compiler_params=_cparams(("parallel", "arbitrary")),
        name="conv_ffn",
    )(h2, h2, h2, x1, w_up_bf, w_up_bf, ffn_dw_w, ffn_dw_w,
      ffn_dw_b.reshape(1, -1), ffn_dw_b.reshape(1, -1), w_down_bf)


def kernel(x, norm1_g, w_in, conv_dw_w, conv_dw_b, conv_ln_g, conv_ln_b, q_norm_g, k_norm_g,
           w_out, norm2_g, w_up, ffn_dw_w, ffn_dw_b, w_down):
    b, s, d = x.shape
    x2 = x.reshape(b * s, d)
    u = _in_proj(x2, norm1_g, w_in.astype(BF16), q_norm_g, k_norm_g)
    u4 = u.reshape(U_SLABS, b, s, LANES)
    y_slabs, w_down_bf = _conv_group(u4, conv_dw_w, conv_dw_b, w_down)
    b_out, w_up_bf = _attn_group(u4, w_up)
    x1, h2 = _out_proj(x2, y_slabs.reshape(CONV_LB, b * s, LANES),
                       b_out.reshape(b * s, ATTN_WIDTH), w_out.astype(BF16),
                       conv_ln_g, conv_ln_b, norm2_g)
    out = _conv_ffn(x1, h2, w_up_bf, ffn_dw_w, ffn_dw_b, w_down_bf, s)
    return out.reshape(b, s, d)
```

```python
import functools

import jax
import jax.numpy as jnp
from jax import lax
from jax.experimental import pallas as pl
from jax.experimental.pallas import tpu as pltpu

D_MODEL = 2048
CONV_CH = 1024
ATTN_WIDTH = 1024
HEAD_DIM = 128
N_HEADS = ATTN_WIDTH // HEAD_DIM
CONV_WIDTH = 31
CONV_HALF = (CONV_WIDTH - 1) // 2
FFN_DIM = 5632
RMS_EPS = 1e-6
LN_EPS = 1e-5
NEG_BIG = -1e30
BAND_R = 64
DILATIONS = (1, 4, 16)
LOG2E = 1.4426950408889634

LANES = 128
BF16_ROWS = 16
VMEM_LIMIT = 56 * 1024 * 1024

F32 = jnp.float32
BF16 = jnp.bfloat16


def _cparams(sem):
    return pltpu.CompilerParams(dimension_semantics=sem, vmem_limit_bytes=VMEM_LIMIT)


IN_TM = 1024
IN_TN = 1024
IN_XPARTS = 4
assert CONV_CH == IN_TN and ATTN_WIDTH == IN_TN
STEP_VAL, STEP_GATE, STEP_Q, STEP_K, STEP_V = range(5)
SLABS_PER_STEP = IN_TN // LANES
U_SLABS = 4 * SLABS_PER_STEP
Q_SLAB0 = SLABS_PER_STEP


def _in_proj_kernel(*refs):
    x_parts = refs[:IN_XPARTS]
    g_ref, w_ref, qg_ref, kg_ref, o_ref, h_scr = refs[IN_XPARTS:]
    j = pl.program_id(1)
    rows = IN_TM // IN_XPARTS

    @pl.when(j == 0)
    def _():
        for p, x_ref in enumerate(x_parts):
            xf = x_ref[...]
            ms = jnp.mean(xf * xf, axis=-1, keepdims=True)
            h_scr[p * rows:(p + 1) * rows, :] = (
                xf * lax.rsqrt(ms + RMS_EPS) * g_ref[...]).astype(BF16)

    def proj():
        return jnp.dot(h_scr[...], w_ref[...], preferred_element_type=F32)

    def store_slabs(res, gain=None):
        for sb in range(SLABS_PER_STEP):
            t = res[:, sb * LANES:(sb + 1) * LANES]
            if gain is not None:
                ms = jnp.mean(t * t, axis=-1, keepdims=True)
                t = t * lax.rsqrt(ms + RMS_EPS) * gain
            o_ref[sb] = t

    @pl.when((j == STEP_VAL) | (j == STEP_V))
    def _():
        store_slabs(proj())

    @pl.when(j == STEP_GATE)
    def _():
        gate = jax.nn.sigmoid(proj())
        for sb in range(SLABS_PER_STEP):
            o_ref[sb] = o_ref[sb] * gate[:, sb * LANES:(sb + 1) * LANES]

    @pl.when(j == STEP_Q)
    def _():
        store_slabs(proj(), qg_ref[...] * (HEAD_DIM ** -0.5 * LOG2E))

    @pl.when(j == STEP_K)
    def _():
        store_slabs(proj(), kg_ref[...])


def _in_proj(x2, g, w_bf, q_norm_g, k_norm_g):
    m, d = x2.shape
    tm, tn = IN_TM, IN_TN
    gain = pl.BlockSpec((1, HEAD_DIM), lambda i, j: (0, 0))
    n_tiles = m // tm
    n_steps = w_bf.shape[1] // tn
    assert n_steps > IN_XPARTS

    def x_part(p):
        def index(i, j):
            nxt = (j + n_steps - 2 - p) // (n_steps - 1)
            return (IN_XPARTS * jnp.minimum(i + nxt, n_tiles - 1) + p, 0)
        return pl.BlockSpec((tm // IN_XPARTS, d), index)

    return pl.pallas_call(
        _in_proj_kernel,
        grid=(n_tiles, n_steps),
        in_specs=[x_part(p) for p in range(IN_XPARTS)] + [
            pl.BlockSpec((1, d), lambda i, j: (0, 0)),
            pl.BlockSpec((d, tn), lambda i, j: (0, j)),
            gain, gain,
        ],
        out_specs=pl.BlockSpec((SLABS_PER_STEP, tm, LANES),
                               lambda i, j: (jnp.maximum(j - 1, 0), i, 0)),
        out_shape=jax.ShapeDtypeStruct((U_SLABS, m, LANES), F32),
        scratch_shapes=[pltpu.VMEM((tm, d), BF16)],
        compiler_params=_cparams(("parallel", "arbitrary")),
        name="in_proj",
    )(*([x2] * IN_XPARTS), g.reshape(1, d), w_bf,
      q_norm_g.reshape(1, HEAD_DIM), k_norm_g.reshape(1, HEAD_DIM))


CONV_TT = 256
CONV_HALO = 16
CONV_RC = 64
CONV_LB = CONV_CH // LANES


def _cast_spec(w, grid):
    rows, cols = w.shape[0] // grid[0], w.shape[1] // grid[1]
    assert (rows * grid[0], cols * grid[1]) == w.shape
    assert rows % BF16_ROWS == 0 and cols % LANES == 0
    return pl.BlockSpec((rows, cols), lambda i, j: (i, j))


def _conv_kernel(a_ref, pa_ref, na_ref, w_ref, b_ref, wf_ref, o_ref, wb_ref, a_scr):
    ti = pl.program_id(1)
    nt = pl.num_programs(1)
    tt = CONV_TT

    wb_ref[...] = wf_ref[...].astype(BF16)

    keep_prev = (ti > 0).astype(F32)
    keep_next = (ti < nt - 1).astype(F32)
    for lb in range(CONV_LB):
        a_scr[lb, 0:CONV_HALO, :] = pa_ref[lb, 0] * keep_prev
        a_scr[lb, CONV_HALO:CONV_HALO + tt, :] = a_ref[lb, 0]
        a_scr[lb, CONV_HALO + tt:, :] = na_ref[lb, 0] * keep_next

    def lane_block(lb, carry):
        for rc in range(tt // CONV_RC):
            r0 = rc * CONV_RC + CONV_HALO - CONV_HALF
            acc = jnp.broadcast_to(b_ref[lb], (CONV_RC, LANES))
            for k in range(CONV_WIDTH):
                acc = acc + a_scr[lb, r0 + k:r0 + k + CONV_RC, :] * w_ref[lb, k:k + 1, :]
            o_ref[lb, 0, rc * CONV_RC:(rc + 1) * CONV_RC, :] = acc
        return carry

    lax.fori_loop(0, CONV_LB, lane_block, 0)


def _conv_group(u4, conv_dw_w, conv_dw_b, w_cast):
    _, b, s, _ = u4.shape
    tt, halo = CONV_TT, CONV_HALO
    nt = s // tt
    hb = tt // halo
    n_hblk = s // halo
    w3 = conv_dw_w.reshape(CONV_WIDTH, CONV_LB, LANES).transpose(1, 0, 2)
    b3 = conv_dw_b.reshape(CONV_LB, 1, LANES)
    full = lambda shape: pl.BlockSpec(shape, lambda bi, ti: (0,) * len(shape))
    cast_spec = _cast_spec(w_cast, (b, nt))
    return pl.pallas_call(
        _conv_kernel,
        grid=(b, nt),
        in_specs=[
            pl.BlockSpec((CONV_LB, 1, tt, LANES), lambda bi, ti: (0, bi, ti, 0)),
            pl.BlockSpec((CONV_LB, 1, halo, LANES),
                         lambda bi, ti: (0, bi, jnp.maximum(ti * hb - 1, 0), 0)),
            pl.BlockSpec((CONV_LB, 1, halo, LANES),
                         lambda bi, ti: (0, bi, jnp.minimum((ti + 1) * hb, n_hblk - 1), 0)),
            full((CONV_LB, CONV_WIDTH, LANES)), full((CONV_LB, 1, LANES)), cast_spec],
        out_specs=[pl.BlockSpec((CONV_LB, 1, tt, LANES), lambda bi, ti: (0, bi, ti, 0)),
                   cast_spec],
        out_shape=[jax.ShapeDtypeStruct((CONV_LB, b, s, LANES), F32),
                   jax.ShapeDtypeStruct(w_cast.shape, BF16)],
        scratch_shapes=[pltpu.VMEM((CONV_LB, tt + 2 * halo, LANES), F32)],
        compiler_params=_cparams(("parallel", "arbitrary")),
        name="conv_group",
    )(u4, u4, u4, w3, b3, w_cast)


ATT_QB = 128
ATT_KB = 256
ATT_NE = (ATT_KB - ATT_QB) // BAND_R + 1
ATT_C4 = 4


def _attn_kernel(slope_ref, q_ref, k_ref, v_ref, wf_ref, o_ref, wb_ref,
                 q4, k4, v4, bias_scr, acc_scr, m_scr, l_scr, out_scr):
    h = pl.program_id(1)
    s_len = q_ref.shape[0]
    cl = s_len // ATT_C4
    slope = slope_ref[h] * LOG2E

    wb_ref[...] = wf_ref[...].astype(BF16)

    for c4 in range(ATT_C4):
        dst = pl.ds(c4 * cl, cl)
        src = pl.ds(c4, cl, stride=ATT_C4)
        q4[dst, :] = q_ref[src, :]
        k4[dst, :] = k_ref[src, :]
        v4[dst, :] = v_ref[src, :]

    rows = lax.broadcasted_iota(jnp.int32, (ATT_QB, ATT_KB), 0)
    cols = lax.broadcasted_iota(jnp.int32, (ATT_QB, ATT_KB), 1)
    for w, dil in enumerate(DILATIONS):
        for e in range(ATT_NE):
            off = jnp.abs(cols - rows - e * BAND_R)
            bias = jnp.where(off <= BAND_R, -(slope * dil) * off.astype(F32), NEG_BIG)
            bias_scr[ATT_NE * w + e] = bias

    ones_rhs = jnp.ones((ATT_KB, LANES), BF16)

    def block(w, qb, kb, vb, out_idx, bias):
        qb, kb, vb = qb.astype(BF16), kb.astype(BF16), vb.astype(BF16)
        sc = lax.dot_general(qb, kb, (((1,), (1,)), ((), ())), preferred_element_type=F32)
        sc = sc + bias
        m = jnp.max(sc, axis=-1, keepdims=True)
        p = jnp.exp2(sc - m).astype(BF16)
        acc = jnp.dot(p, jnp.concatenate([vb, ones_rhs[:vb.shape[0]]], axis=1),
                      preferred_element_type=F32)
        acc_scr[w, out_idx, :] = acc[:, :HEAD_DIM]
        l_scr[w, out_idx, :] = acc[:, HEAD_DIM:]
        m_scr[w, out_idx, :] = jnp.broadcast_to(m, (ATT_QB, LANES))

    def key_start(q0, class_len, nk):
        return min(max(q0 - BAND_R, 0), class_len - nk)

    for i in range(s_len // ATT_QB):
        q0 = i * ATT_QB
        k0 = key_start(q0, s_len, ATT_KB)
        qi, ki = pl.ds(q0, ATT_QB), pl.ds(k0, ATT_KB)
        block(0, q_ref[qi, :], k_ref[ki, :], v_ref[ki, :], qi,
              bias_scr[(q0 - k0) // BAND_R])

    for c4 in range(ATT_C4):
        for i in range(cl // ATT_QB):
            q0 = i * ATT_QB
            k0 = key_start(q0, cl, ATT_KB)
            qi, ki = pl.ds(c4 * cl + q0, ATT_QB), pl.ds(c4 * cl + k0, ATT_KB)
            block(1, q4[qi, :], k4[ki, :], v4[ki, :], qi,
                  bias_scr[ATT_NE + (q0 - k0) // BAND_R])

    sub_len = s_len // DILATIONS[2]
    step4 = DILATIONS[2] // ATT_C4
    for c4 in range(ATT_C4):
        for c in range(step4):
            ki = pl.ds(c4 * cl + c, sub_len, stride=step4)
            for q0 in range(0, sub_len, ATT_QB):
                qi = pl.ds(c4 * cl + c + step4 * q0, ATT_QB, stride=step4)
                block(2, q4[qi, :], k4[ki, :], v4[ki, :], qi,
                      bias_scr[2 * ATT_NE + q0 // BAND_R][:, :sub_len])

    chunk = 256
    for c4 in range(ATT_C4):
        for r in range(cl // chunk):
            nat = pl.ds(c4 + ATT_C4 * r * chunk, chunk, stride=ATT_C4)
            grp = pl.ds(c4 * cl + r * chunk, chunk)
            m0, m1, m2 = m_scr[0, nat, :], m_scr[1, grp, :], m_scr[2, grp, :]
            mm = jnp.maximum(jnp.maximum(m0, m1), m2)
            a0, a1, a2 = jnp.exp2(m0 - mm), jnp.exp2(m1 - mm), jnp.exp2(m2 - mm)
            num = a0 * acc_scr[0, nat, :] + a1 * acc_scr[1, grp, :] + a2 * acc_scr[2, grp, :]
            den = a0 * l_scr[0, nat, :] + a1 * l_scr[1, grp, :] + a2 * l_scr[2, grp, :]
            out_scr[nat, :] = num / den
    o_ref[0] = out_scr[...].astype(o_ref.dtype)


def _attn_group(u4, w_cast):
    _, b, s, _ = u4.shape
    slopes = jnp.asarray([2.0 ** (-8.0 * (i + 1) / N_HEADS) for i in range(N_HEADS)], F32)
    head = lambda base: pl.BlockSpec((None, None, s, HEAD_DIM),
                                     lambda bi, hi: (base + hi, bi, 0, 0))
    cast_spec = _cast_spec(w_cast, (b, N_HEADS))
    return pl.pallas_call(
        _attn_kernel,
        grid=(b, N_HEADS),
        in_specs=[pl.BlockSpec(memory_space=pltpu.SMEM),
                  head(Q_SLAB0), head(Q_SLAB0 + N_HEADS), head(Q_SLAB0 + 2 * N_HEADS),
                  cast_spec],
        out_specs=[pl.BlockSpec((1, s, HEAD_DIM), lambda bi, hi: (bi, 0, hi)), cast_spec],
        out_shape=[jax.ShapeDtypeStruct((b, s, ATTN_WIDTH), BF16),
                   jax.ShapeDtypeStruct(w_cast.shape, BF16)],
        scratch_shapes=[pltpu.VMEM((s, HEAD_DIM), F32),
                        pltpu.VMEM((s, HEAD_DIM), F32),
                        pltpu.VMEM((s, HEAD_DIM), F32),
                        pltpu.VMEM((len(DILATIONS) * ATT_NE, ATT_QB, ATT_KB), F32),
                        pltpu.VMEM((3, s, HEAD_DIM), F32),
                        pltpu.VMEM((3, s, LANES), F32),
                        pltpu.VMEM((3, s, LANES), F32),
                        pltpu.VMEM((s, HEAD_DIM), F32)],
        compiler_params=_cparams(("parallel", "arbitrary")),
        name="attn_group",
    )(slopes, u4, u4, u4, w_cast)


def _out_proj_kernel(x_ref, y_ref, b_ref, w_ref, lg_ref, lb_ref, g_ref, o_ref, h_ref):
    acc = jnp.dot(b_ref[...], w_ref[CONV_CH:, :], preferred_element_type=F32)

    ys = [y_ref[lb] for lb in range(CONV_LB)]
    tot = ys[0]
    for t in ys[1:]:
        tot = tot + t
    mu = jnp.sum(tot, axis=-1, keepdims=True) * (1.0 / CONV_CH)
    sq = None
    for t in ys:
        c = t - mu
        sq = c * c if sq is None else sq + c * c
    var = jnp.sum(sq, axis=-1, keepdims=True) * (1.0 / CONV_CH)
    rstd = lax.rsqrt(var + LN_EPS)
    a_cols = []
    for lb, t in enumerate(ys):
        ls = slice(lb * LANES, (lb + 1) * LANES)
        z = (t - mu) * rstd * lg_ref[:, ls] + lb_ref[:, ls]
        a_cols.append((z * jax.nn.sigmoid(z)).astype(BF16))
    a = jnp.concatenate(a_cols, axis=1)

    acc = acc + jnp.dot(a, w_ref[0:CONV_CH, :], preferred_element_type=F32)
    x1 = x_ref[...] + acc
    o_ref[...] = x1
    ms = jnp.mean(x1 * x1, axis=-1, keepdims=True)
    h_ref[...] = (x1 * lax.rsqrt(ms + RMS_EPS) * g_ref[...]).astype(BF16)


def _out_proj(x2, y_slabs, b2, w_bf, conv_ln_g, conv_ln_b, norm2_g, tm=512):
    m, d = x2.shape
    row = lambda shape: pl.BlockSpec(shape, lambda i: (0,) * len(shape))
    return pl.pallas_call(
        _out_proj_kernel,
        grid=(m // tm,),
        in_specs=[
            pl.BlockSpec((tm, d), lambda i: (i, 0)),
            pl.BlockSpec((CONV_LB, tm, LANES), lambda i: (0, i, 0)),
            pl.BlockSpec((tm, ATTN_WIDTH), lambda i: (i, 0)),
            row((d, d)), row((1, CONV_CH)), row((1, CONV_CH)), row((1, d)),
        ],
        out_specs=[pl.BlockSpec((tm, d), lambda i: (i, 0)),
                   pl.BlockSpec((tm, d), lambda i: (i, 0))],
        out_shape=[jax.ShapeDtypeStruct((m, d), F32),
                   jax.ShapeDtypeStruct((m, d), BF16)],
        compiler_params=_cparams(("parallel",)),
        name="out_proj",
    )(x2, y_slabs, b2, w_bf, conv_ln_g.reshape(1, CONV_CH), conv_ln_b.reshape(1, CONV_CH),
      norm2_g.reshape(1, d))


FFN_TM = 1024
FFN_TF = 512
FFN_PAD = BF16_ROWS
FFN_XC = 256
FFN_NXC = D_MODEL // FFN_XC


def _ffn_kernel(h_ref, hp_ref, hn_ref, x_ref, wg_ref, wv_ref, cwg_ref, cwv_ref,
                cbg_ref, cbv_ref, wd_ref, o_ref, h_scr, ug_scr, uv_scr, *, tiles_per_seq):
    i = pl.program_id(0)
    j = pl.program_id(1)
    tm = FFN_TM

    @pl.when(j == 0)
    def _():
        keep_prev = ((i % tiles_per_seq) != 0).astype(F32)
        keep_next = ((i % tiles_per_seq) != tiles_per_seq - 1).astype(F32)
        rid = lax.broadcasted_iota(jnp.int32, (FFN_PAD, 1), 0)
        hp = hp_ref[...].astype(F32)[FFN_PAD - 1:FFN_PAD, :] * keep_prev
        hn = hn_ref[...].astype(F32)[0:1, :] * keep_next
        h_scr[0:FFN_PAD, :] = jnp.where(rid == FFN_PAD - 1, hp, 0.0).astype(BF16)
        h_scr[FFN_PAD:FFN_PAD + tm, :] = h_ref[...]
        h_scr[FFN_PAD + tm:, :] = jnp.where(rid == 0, hn, 0.0).astype(BF16)
        o_ref[...] = jnp.zeros_like(o_ref)

    for c in range(FFN_NXC):
        @pl.when(j == c)
        def _(c=c):
            o_ref[:, c * FFN_XC:(c + 1) * FFN_XC] += x_ref[...]

    hh = h_scr[...]

    def up_conv(w_ref, cw_ref, cb_ref, u_scr):
        u = jnp.dot(hh, w_ref[...], preferred_element_type=F32)
        cols = []
        for c in range(FFN_TF // LANES):
            ls = slice(c * LANES, (c + 1) * LANES)
            u_scr[c] = u[:, ls]
            cols.append(cw_ref[0:1, ls] * u_scr[c, FFN_PAD - 1:FFN_PAD - 1 + tm, :]
                        + cw_ref[1:2, ls] * u_scr[c, FFN_PAD:FFN_PAD + tm, :]
                        + cw_ref[2:3, ls] * u_scr[c, FFN_PAD + 1:FFN_PAD + 1 + tm, :]
                        + cb_ref[:, ls])
        return jnp.concatenate(cols, axis=1)

    gte = up_conv(wg_ref, cwg_ref, cbg_ref, ug_scr)
    val = up_conv(wv_ref, cwv_ref, cbv_ref, uv_scr)
    act = (gte * jax.nn.sigmoid(gte) * val).astype(BF16)
    o_ref[...] += jnp.dot(act, wd_ref[...], preferred_element_type=F32)


def _conv_ffn(x1, h2, w_up_bf, ffn_dw_w, ffn_dw_b, w_down_bf, seq_len):
    m, d = x1.shape
    tm, tf = FFN_TM, FFN_TF
    nf = FFN_DIM // tf
    assert nf >= FFN_NXC
    hb = tm // FFN_PAD
    n_hblk = m // FFN_PAD
    kern = functools.partial(_ffn_kernel, tiles_per_seq=seq_len // tm)
    return pl.pallas_call(
        kern,
        grid=(m // tm, nf),
        in_specs=[
            pl.BlockSpec((tm, d), lambda i, j: (i, 0)),
            pl.BlockSpec((FFN_PAD, d), lambda i, j: (jnp.maximum(i * hb - 1, 0), 0)),
            pl.BlockSpec((FFN_PAD, d), lambda i, j: (jnp.minimum((i + 1) * hb, n_hblk - 1), 0)),
            pl.BlockSpec((tm, FFN_XC), lambda i, j: (i, jnp.minimum(j, FFN_NXC - 1))),
            pl.BlockSpec((d, tf), lambda i, j: (0, j)),
            pl.BlockSpec((d, tf), lambda i, j: (0, nf + j)),
            pl.BlockSpec((3, tf), lambda i, j: (0, j)),
            pl.BlockSpec((3, tf), lambda i, j: (0, nf + j)),
            pl.BlockSpec((1, tf), lambda i, j: (0, j)),
            pl.BlockSpec((1, tf), lambda i, j: (0, nf + j)),
            pl.BlockSpec((tf, d), lambda i, j: (j, 0)),
        ],
        out_specs=pl.BlockSpec((tm, d), lambda i, j: (i, 0)),
        out_shape=jax.ShapeDtypeStruct((m, d), F32),
        scratch_shapes=[pltpu.VMEM((tm + 2 * FFN_PAD, d), BF16),
                        pltpu.VMEM((tf // LANES, tm + 2 * FFN_PAD, LANES), F32),
                        pltpu.VMEM((tf // LANES, tm + 2 * FFN_PAD, LANES), F32)],
        compiler_params=_cparams(("parallel", "arbitrary")),
        name="conv_ffn",
    )(h2, h2, h2, x1, w_up_bf, w_up_bf, ffn_dw_w, ffn_dw_w,
      ffn_dw_b.reshape(1, -1), ffn_dw_b.reshape(1, -1), w_down_bf)


def kernel(x, norm1_g, w_in, conv_dw_w, conv_dw_b, conv_ln_g, conv_ln_b, q_norm_g, k_norm_g,
           w_out, norm2_g, w_up, ffn_dw_w, ffn_dw_b, w_down):
    b, s, d = x.shape
    x2 = x.reshape(b * s, d)
    u = _in_proj(x2, norm1_g, w_in.astype(BF16), q_norm_g, k_norm_g)
    u4 = u.reshape(U_SLABS, b, s, LANES)
    y_slabs, w_down_bf = _conv_group(u4, conv_dw_w, conv_dw_b, w_down)
    b_out, w_up_bf = _attn_group(u4, w_up)
    x1, h2 = _out_proj(x2, y_slabs.reshape(CONV_LB, b * s, LANES),
                       b_out.reshape(b * s, ATTN_WIDTH), w_out.astype(BF16),
                       conv_ln_g, conv_ln_b, norm2_g)
    out = _conv_ffn(x1, h2, w_up_bf, ffn_dw_w, ffn_dw_b, w_down_bf, s)
    return out.reshape(b, s, d)
```

```python
import functools

import jax
import jax.numpy as jnp
from jax import lax
from jax.experimental import pallas as pl
from jax.experimental.pallas import tpu as pltpu

D_MODEL = 2048
CONV_CH = 1024
ATTN_WIDTH = 1024
HEAD_DIM = 128
N_HEADS = ATTN_WIDTH // HEAD_DIM
CONV_WIDTH = 31
CONV_HALF = (CONV_WIDTH - 1) // 2
FFN_DIM = 5632
RMS_EPS = 1e-6
LN_EPS = 1e-5
NEG_BIG = -1e30
BAND_R = 64
DILATIONS = (1, 4, 16)
LOG2E = 1.4426950408889634

LANES = 128
BF16_ROWS = 16
VMEM_LIMIT = 56 * 1024 * 1024

F32 = jnp.float32
BF16 = jnp.bfloat16


def _cparams(sem):
    return pltpu.CompilerParams(dimension_semantics=sem, vmem_limit_bytes=VMEM_LIMIT)


IN_TM = 1024
IN_TN = 1024
IN_XPARTS = 4
assert CONV_CH == IN_TN and ATTN_WIDTH == IN_TN
STEP_VAL, STEP_GATE, STEP_Q, STEP_K, STEP_V = range(5)
SLABS_PER_STEP = IN_TN // LANES
U_SLABS = 4 * SLABS_PER_STEP
Q_SLAB0 = SLABS_PER_STEP


def _in_proj_kernel(*refs):
    x_parts = refs[:IN_XPARTS]
    g_ref, w_ref, qg_ref, kg_ref, wof_ref, o_ref, wob_ref, h_scr = refs[IN_XPARTS:]
    j = pl.program_id(1)
    rows = IN_TM // IN_XPARTS

    wob_ref[...] = wof_ref[...].astype(BF16)

    @pl.when(j == 0)
    def _():
        for p, x_ref in enumerate(x_parts):
            xf = x_ref[...]
            ms = jnp.mean(xf * xf, axis=-1, keepdims=True)
            h_scr[p * rows:(p + 1) * rows, :] = (
                xf * lax.rsqrt(ms + RMS_EPS) * g_ref[...]).astype(BF16)

    def proj():
        return jnp.dot(h_scr[...], w_ref[...], preferred_element_type=F32)

    def store_slabs(res, gain=None):
        for sb in range(SLABS_PER_STEP):
            t = res[:, sb * LANES:(sb + 1) * LANES]
            if gain is not None:
                ms = jnp.mean(t * t, axis=-1, keepdims=True)
                t = t * lax.rsqrt(ms + RMS_EPS) * gain
            o_ref[sb] = t

    @pl.when((j == STEP_VAL) | (j == STEP_V))
    def _():
        store_slabs(proj())

    @pl.when(j == STEP_GATE)
    def _():
        gate = jax.nn.sigmoid(proj())
        for sb in range(SLABS_PER_STEP):
            o_ref[sb] = o_ref[sb] * gate[:, sb * LANES:(sb + 1) * LANES]

    @pl.when(j == STEP_Q)
    def _():
        store_slabs(proj(), qg_ref[...] * (HEAD_DIM ** -0.5 * LOG2E))

    @pl.when(j == STEP_K)
    def _():
        store_slabs(proj(), kg_ref[...])


def _in_proj(x2, g, w_bf, q_norm_g, k_norm_g, w_cast):
    m, d = x2.shape
    tm, tn = IN_TM, IN_TN
    gain = pl.BlockSpec((1, HEAD_DIM), lambda i, j: (0, 0))
    n_tiles = m // tm
    n_steps = w_bf.shape[1] // tn
    assert n_steps > IN_XPARTS
    cast_rows, cast_cols = w_cast.shape[0] // n_tiles, w_cast.shape[1] // (n_steps - 1)
    assert (cast_rows * n_tiles, cast_cols * (n_steps - 1)) == w_cast.shape
    assert cast_rows % BF16_ROWS == 0 and cast_cols % LANES == 0
    cast_spec = pl.BlockSpec((cast_rows, cast_cols),
                             lambda i, j: (i, jnp.minimum(j, n_steps - 2)))

    def x_part(p):
        def index(i, j):
            nxt = (j + n_steps - 2 - p) // (n_steps - 1)
            return (IN_XPARTS * jnp.minimum(i + nxt, n_tiles - 1) + p, 0)
        return pl.BlockSpec((tm // IN_XPARTS, d), index)

    return pl.pallas_call(
        _in_proj_kernel,
        grid=(n_tiles, n_steps),
        in_specs=[x_part(p) for p in range(IN_XPARTS)] + [
            pl.BlockSpec((1, d), lambda i, j: (0, 0)),
            pl.BlockSpec((d, tn), lambda i, j: (0, j)),
            gain, gain, cast_spec,
        ],
        out_specs=[pl.BlockSpec((SLABS_PER_STEP, tm, LANES),
                                lambda i, j: (jnp.maximum(j - 1, 0), i, 0)),
                   cast_spec],
        out_shape=[jax.ShapeDtypeStruct((U_SLABS, m, LANES), F32),
                   jax.ShapeDtypeStruct(w_cast.shape, BF16)],
        scratch_shapes=[pltpu.VMEM((tm, d), BF16)],
        compiler_params=_cparams(("parallel", "arbitrary")),
        name="in_proj",
    )(*([x2] * IN_XPARTS), g.reshape(1, d), w_bf,
      q_norm_g.reshape(1, HEAD_DIM), k_norm_g.reshape(1, HEAD_DIM), w_cast)


CONV_TT = 256
CONV_HALO = 16
CONV_RC = 64
CONV_LB = CONV_CH // LANES


def _cast_spec(w, grid):
    rows, cols = w.shape[0] // grid[0], w.shape[1] // grid[1]
    assert (rows * grid[0], cols * grid[1]) == w.shape
    assert rows % BF16_ROWS == 0 and cols % LANES == 0
    return pl.BlockSpec((rows, cols), lambda i, j: (i, j))


def _conv_kernel(a_ref, pa_ref, na_ref, w_ref, b_ref, wf_ref, o_ref, wb_ref, a_scr):
    ti = pl.program_id(1)
    nt = pl.num_programs(1)
    tt = CONV_TT

    wb_ref[...] = wf_ref[...].astype(BF16)

    keep_prev = (ti > 0).astype(F32)
    keep_next = (ti < nt - 1).astype(F32)
    for lb in range(CONV_LB):
        a_scr[lb, 0:CONV_HALO, :] = pa_ref[lb, 0] * keep_prev
        a_scr[lb, CONV_HALO:CONV_HALO + tt, :] = a_ref[lb, 0]
        a_scr[lb, CONV_HALO + tt:, :] = na_ref[lb, 0] * keep_next

    def lane_block(lb, carry):
        for rc in range(tt // CONV_RC):
            r0 = rc * CONV_RC + CONV_HALO - CONV_HALF
            acc = jnp.broadcast_to(b_ref[lb], (CONV_RC, LANES))
            for k in range(CONV_WIDTH):
                acc = acc + a_scr[lb, r0 + k:r0 + k + CONV_RC, :] * w_ref[lb, k:k + 1, :]
            o_ref[lb, 0, rc * CONV_RC:(rc + 1) * CONV_RC, :] = acc
        return carry

    lax.fori_loop(0, CONV_LB, lane_block, 0)


def _conv_group(u4, conv_dw_w, conv_dw_b, w_cast):
    _, b, s, _ = u4.shape
    tt, halo = CONV_TT, CONV_HALO
    nt = s // tt
    hb = tt // halo
    n_hblk = s // halo
    w3 = conv_dw_w.reshape(CONV_WIDTH, CONV_LB, LANES).transpose(1, 0, 2)
    b3 = conv_dw_b.reshape(CONV_LB, 1, LANES)
    full = lambda shape: pl.BlockSpec(shape, lambda bi, ti: (0,) * len(shape))
    cast_spec = _cast_spec(w_cast, (b, nt))
    return pl.pallas_call(
        _conv_kernel,
        grid=(b, nt),
        in_specs=[
            pl.BlockSpec((CONV_LB, 1, tt, LANES), lambda bi, ti: (0, bi, ti, 0)),
            pl.BlockSpec((CONV_LB, 1, halo, LANES),
                         lambda bi, ti: (0, bi, jnp.maximum(ti * hb - 1, 0), 0)),
            pl.BlockSpec((CONV_LB, 1, halo, LANES),
                         lambda bi, ti: (0, bi, jnp.minimum((ti + 1) * hb, n_hblk - 1), 0)),
            full((CONV_LB, CONV_WIDTH, LANES)), full((CONV_LB, 1, LANES)), cast_spec],
        out_specs=[pl.BlockSpec((CONV_LB, 1, tt, LANES), lambda bi, ti: (0, bi, ti, 0)),
                   cast_spec],
        out_shape=[jax.ShapeDtypeStruct((CONV_LB, b, s, LANES), F32),
                   jax.ShapeDtypeStruct(w_cast.shape, BF16)],
        scratch_shapes=[pltpu.VMEM((CONV_LB, tt + 2 * halo, LANES), F32)],
        compiler_params=_cparams(("parallel", "arbitrary")),
        name="conv_group",
    )(u4, u4, u4, w3, b3, w_cast)


ATT_QB = 128
ATT_KB = 256
ATT_NE = (ATT_KB - ATT_QB) // BAND_R + 1
ATT_C4 = 4


def _attn_kernel(slope_ref, q_ref, k_ref, v_ref, wf_ref, o_ref, wb_ref,
                 q4, k4, v4, bias_scr, acc_scr, m_scr, l_scr, out_scr):
    h = pl.program_id(1)
    s_len = q_ref.shape[0]
    cl = s_len // ATT_C4
    slope = slope_ref[h] * LOG2E

    wb_ref[...] = wf_ref[...].astype(BF16)

    for c4 in range(ATT_C4):
        dst = pl.ds(c4 * cl, cl)
        src = pl.ds(c4, cl, stride=ATT_C4)
        q4[dst, :] = q_ref[src, :]
        k4[dst, :] = k_ref[src, :]
        v4[dst, :] = v_ref[src, :]

    rows = lax.broadcasted_iota(jnp.int32, (ATT_QB, ATT_KB), 0)
    cols = lax.broadcasted_iota(jnp.int32, (ATT_QB, ATT_KB), 1)
    for w, dil in enumerate(DILATIONS):
        for e in range(ATT_NE):
            off = jnp.abs(cols - rows - e * BAND_R)
            bias = jnp.where(off <= BAND_R, -(slope * dil) * off.astype(F32), NEG_BIG)
            bias_scr[ATT_NE * w + e] = bias

    ones_rhs = jnp.ones((ATT_KB, LANES), BF16)

    def block(w, qb, kb, vb, out_idx, bias):
        qb, kb, vb = qb.astype(BF16), kb.astype(BF16), vb.astype(BF16)
        sc = lax.dot_general(qb, kb, (((1,), (1,)), ((), ())), preferred_element_type=F32)
        sc = sc + bias
        m = jnp.max(sc, axis=-1, keepdims=True)
        p = jnp.exp2(sc - m).astype(BF16)
        acc = jnp.dot(p, jnp.concatenate([vb, ones_rhs[:vb.shape[0]]], axis=1),
                      preferred_element_type=F32)
        acc_scr[w, out_idx, :] = acc[:, :HEAD_DIM]
        l_scr[w, out_idx, :] = acc[:, HEAD_DIM:]
        m_scr[w, out_idx, :] = jnp.broadcast_to(m, (ATT_QB, LANES))

    def key_start(q0, class_len, nk):
        return min(max(q0 - BAND_R, 0), class_len - nk)

    for i in range(s_len // ATT_QB):
        q0 = i * ATT_QB
        k0 = key_start(q0, s_len, ATT_KB)
        qi, ki = pl.ds(q0, ATT_QB), pl.ds(k0, ATT_KB)
        block(0, q_ref[qi, :], k_ref[ki, :], v_ref[ki, :], qi,
              bias_scr[(q0 - k0) // BAND_R])

    for c4 in range(ATT_C4):
        for i in range(cl // ATT_QB):
            q0 = i * ATT_QB
            k0 = key_start(q0, cl, ATT_KB)
            qi, ki = pl.ds(c4 * cl + q0, ATT_QB), pl.ds(c4 * cl + k0, ATT_KB)
            block(1, q4[qi, :], k4[ki, :], v4[ki, :], qi,
                  bias_scr[ATT_NE + (q0 - k0) // BAND_R])

    sub_len = s_len // DILATIONS[2]
    step4 = DILATIONS[2] // ATT_C4
    for c4 in range(ATT_C4):
        for c in range(step4):
            ki = pl.ds(c4 * cl + c, sub_len, stride=step4)
            for q0 in range(0, sub_len, ATT_QB):
                qi = pl.ds(c4 * cl + c + step4 * q0, ATT_QB, stride=step4)
                block(2, q4[qi, :], k4[ki, :], v4[ki, :], qi,
                      bias_scr[2 * ATT_NE + q0 // BAND_R][:, :sub_len])

    chunk = 256
    for c4 in range(ATT_C4):
        for r in range(cl // chunk):
            nat = pl.ds(c4 + ATT_C4 * r * chunk, chunk, stride=ATT_C4)
            grp = pl.ds(c4 * cl + r * chunk, chunk)
            m0, m1, m2 = m_scr[0, nat, :], m_scr[1, grp, :], m_scr[2, grp, :]
            mm = jnp.maximum(jnp.maximum(m0, m1), m2)
            a0, a1, a2 = jnp.exp2(m0 - mm), jnp.exp2(m1 - mm), jnp.exp2(m2 - mm)
            num = a0 * acc_scr[0, nat, :] + a1 * acc_scr[1, grp, :] + a2 * acc_scr[2, grp, :]
            den = a0 * l_scr[0, nat, :] + a1 * l_scr[1, grp, :] + a2 * l_scr[2, grp, :]
            out_scr[nat, :] = num / den
    o_ref[0] = out_scr[...].astype(o_ref.dtype)


def _attn_group(u4, w_cast):
    _, b, s, _ = u4.shape
    slopes = jnp.asarray([2.0 ** (-8.0 * (i + 1) / N_HEADS) for i in range(N_HEADS)], F32)
    head = lambda base: pl.BlockSpec((None, None, s, HEAD_DIM),
                                     lambda bi, hi: (base + hi, bi, 0, 0))
    cast_spec = _cast_spec(w_cast, (b, N_HEADS))
    return pl.pallas_call(
        _attn_kernel,
        grid=(b, N_HEADS),
        in_specs=[pl.BlockSpec(memory_space=pltpu.SMEM),
                  head(Q_SLAB0), head(Q_SLAB0 + N_HEADS), head(Q_SLAB0 + 2 * N_HEADS),
                  cast_spec],
        out_specs=[pl.BlockSpec((1, s, HEAD_DIM), lambda bi, hi: (bi, 0, hi)), cast_spec],
        out_shape=[jax.ShapeDtypeStruct((b, s, ATTN_WIDTH), BF16),
                   jax.ShapeDtypeStruct(w_cast.shape, BF16)],
        scratch_shapes=[pltpu.VMEM((s, HEAD_DIM), F32),
                        pltpu.VMEM((s, HEAD_DIM), F32),
                        pltpu.VMEM((s, HEAD_DIM), F32),
                        pltpu.VMEM((len(DILATIONS) * ATT_NE, ATT_QB, ATT_KB), F32),
                        pltpu.VMEM((3, s, HEAD_DIM), F32),
                        pltpu.VMEM((3, s, LANES), F32),
                        pltpu.VMEM((3, s, LANES), F32),
                        pltpu.VMEM((s, HEAD_DIM), F32)],
        compiler_params=_cparams(("parallel", "arbitrary")),
        name="attn_group",
    )(slopes, u4, u4, u4, w_cast)


def _out_proj_kernel(x_ref, y_ref, b_ref, w_ref, lg_ref, lb_ref, g_ref, o_ref, h_ref):
    acc = jnp.dot(b_ref[...], w_ref[CONV_CH:, :], preferred_element_type=F32)

    ys = [y_ref[lb] for lb in range(CONV_LB)]
    tot = ys[0]
    for t in ys[1:]:
        tot = tot + t
    mu = jnp.sum(tot, axis=-1, keepdims=True) * (1.0 / CONV_CH)
    sq = None
    for t in ys:
        c = t - mu
        sq = c * c if sq is None else sq + c * c
    var = jnp.sum(sq, axis=-1, keepdims=True) * (1.0 / CONV_CH)
    rstd = lax.rsqrt(var + LN_EPS)
    a_cols = []
    for lb, t in enumerate(ys):
        ls = slice(lb * LANES, (lb + 1) * LANES)
        z = (t - mu) * rstd * lg_ref[:, ls] + lb_ref[:, ls]
        a_cols.append((z * jax.nn.sigmoid(z)).astype(BF16))
    a = jnp.concatenate(a_cols, axis=1)

    acc = acc + jnp.dot(a, w_ref[0:CONV_CH, :], preferred_element_type=F32)
    x1 = x_ref[...] + acc
    o_ref[...] = x1
    ms = jnp.mean(x1 * x1, axis=-1, keepdims=True)
    h_ref[...] = (x1 * lax.rsqrt(ms + RMS_EPS) * g_ref[...]).astype(BF16)


def _out_proj(x2, y_slabs, b2, w_bf, conv_ln_g, conv_ln_b, norm2_g, tm=512):
    m, d = x2.shape
    row = lambda shape: pl.BlockSpec(shape, lambda i: (0,) * len(shape))
    return pl.pallas_call(
        _out_proj_kernel,
        grid=(m // tm,),
        in_specs=[
            pl.BlockSpec((tm, d), lambda i: (i, 0)),
            pl.BlockSpec((CONV_LB, tm, LANES), lambda i: (0, i, 0)),
            pl.BlockSpec((tm, ATTN_WIDTH), lambda i: (i, 0)),
            row((d, d)), row((1, CONV_CH)), row((1, CONV_CH)), row((1, d)),
        ],
        out_specs=[pl.BlockSpec((tm, d), lambda i: (i, 0)),
                   pl.BlockSpec((tm, d), lambda i: (i, 0))],
        out_shape=[jax.ShapeDtypeStruct((m, d), F32),
                   jax.ShapeDtypeStruct((m, d), BF16)],
        compiler_params=_cparams(("parallel",)),
        name="out_proj",
    )(x2, y_slabs, b2, w_bf, conv_ln_g.reshape(1, CONV_CH), conv_ln_b.reshape(1, CONV_CH),
      norm2_g.reshape(1, d))


FFN_TM = 1024
FFN_TF = 512
FFN_PAD = BF16_ROWS
FFN_XC = 256
FFN_NXC = D_MODEL // FFN_XC


def _ffn_kernel(h_ref, hp_ref, hn_ref, x_ref, wg_ref, wv_ref, cwg_ref, cwv_ref,
                cbg_ref, cbv_ref, wd_ref, o_ref, h_scr, ug_scr, uv_scr, *, tiles_per_seq):
    i = pl.program_id(0)
    j = pl.program_id(1)
    tm = FFN_TM

    @pl.when(j == 0)
    def _():
        keep_prev = ((i % tiles_per_seq) != 0).astype(F32)
        keep_next = ((i % tiles_per_seq) != tiles_per_seq - 1).astype(F32)
        rid = lax.broadcasted_iota(jnp.int32, (FFN_PAD, 1), 0)
        hp = hp_ref[...].astype(F32)[FFN_PAD - 1:FFN_PAD, :] * keep_prev
        hn = hn_ref[...].astype(F32)[0:1, :] * keep_next
        h_scr[0:FFN_PAD, :] = jnp.where(rid == FFN_PAD - 1, hp, 0.0).astype(BF16)
        h_scr[FFN_PAD:FFN_PAD + tm, :] = h_ref[...]
        h_scr[FFN_PAD + tm:, :] = jnp.where(rid == 0, hn, 0.0).astype(BF16)
        o_ref[...] = jnp.zeros_like(o_ref)

    for c in range(FFN_NXC):
        @pl.when(j == c)
        def _(c=c):
            o_ref[:, c * FFN_XC:(c + 1) * FFN_XC] += x_ref[...]

    hh = h_scr[...]

    def up_conv(w_ref, cw_ref, cb_ref, u_scr):
        u = jnp.dot(hh, w_ref[...], preferred_element_type=F32)
        cols = []
        for c in range(FFN_TF // LANES):
            ls = slice(c * LANES, (c + 1) * LANES)
            u_scr[c] = u[:, ls]
            cols.append(cw_ref[0:1, ls] * u_scr[c, FFN_PAD - 1:FFN_PAD - 1 + tm, :]
                        + cw_ref[1:2, ls] * u_scr[c, FFN_PAD:FFN_PAD + tm, :]
                        + cw_ref[2:3, ls] * u_scr[c, FFN_PAD + 1:FFN_PAD + 1 + tm, :]
                        + cb_ref[:, ls])
        return jnp.concatenate(cols, axis=1)

    gte = up_conv(wg_ref, cwg_ref, cbg_ref, ug_scr)
    val = up_conv(wv_ref, cwv_ref, cbv_ref, uv_scr)
    act = (gte * jax.nn.sigmoid(gte) * val).astype(BF16)
    o_ref[...] += jnp.dot(act, wd_ref[...], preferred_element_type=F32)


def _conv_ffn(x1, h2, w_up_bf, ffn_dw_w, ffn_dw_b, w_down_bf, seq_len):
    m, d = x1.shape
    tm, tf = FFN_TM, FFN_TF
    nf = FFN_DIM // tf
    assert nf >= FFN_NXC
    hb = tm // FFN_PAD
    n_hblk = m // FFN_PAD
    kern = functools.partial(_ffn_kernel, tiles_per_seq=seq_len // tm)
    return pl.pallas_call(
        kern,
        grid=(m // tm, nf),
        in_specs=[
            pl.BlockSpec((tm, d), lambda i, j: (i, 0)),
            pl.BlockSpec((FFN_PAD, d), lambda i, j: (jnp.maximum(i * hb - 1, 0), 0)),
            pl.BlockSpec((FFN_PAD, d), lambda i, j: (jnp.minimum((i + 1) * hb, n_hblk - 1), 0)),
            pl.BlockSpec((tm, FFN_XC), lambda i, j: (i, jnp.minimum(j, FFN_NXC - 1))),
            pl.BlockSpec((d, tf), lambda i, j: (0, j)),
            pl.BlockSpec((d, tf), lambda i, j: (0, nf + j)),
            pl.BlockSpec((3, tf), lambda i, j: (0, j)),
            pl.BlockSpec((3, tf), lambda i, j: (0, nf + j)),
            pl.BlockSpec((1, tf), lambda i, j: (0, j)),
            pl.BlockSpec((1, tf), lambda i, j: (0, nf + j)),
            pl.BlockSpec((tf, d), lambda i, j: (j, 0)),
        ],
        out_specs=pl.BlockSpec((tm, d), lambda i, j: (i, 0)),
        out_shape=jax.ShapeDtypeStruct((m, d), F32),
        scratch_shapes=[pltpu.VMEM((tm + 2 * FFN_PAD, d), BF16),
                        pltpu.VMEM((tf // LANES, tm + 2 * FFN_PAD, LANES), F32),
                        pltpu.VMEM((tf // LANES, tm + 2 * FFN_PAD, LANES), F32)],
        compiler_params=_cparams(("parallel", "arbitrary")),
        name="conv_ffn",
    )(h2, h2, h2, x1, w_up_bf, w_up_bf, ffn_dw_w, ffn_dw_w,
      ffn_dw_b.reshape(1, -1), ffn_dw_b.reshape(1, -1), w_down_bf)


def kernel(x, norm1_g, w_in, conv_dw_w, conv_dw_b, conv_ln_g, conv_ln_b, q_norm_g, k_norm_g,
           w_out, norm2_g, w_up, ffn_dw_w, ffn_dw_b, w_down):
    b, s, d = x.shape
    x2 = x.reshape(b * s, d)
    u, w_out_bf = _in_proj(x2, norm1_g, w_in.astype(BF16), q_norm_g, k_norm_g, w_out)
    u4 = u.reshape(U_SLABS, b, s, LANES)
    y_slabs, w_down_bf = _conv_group(u4, conv_dw_w, conv_dw_b, w_down)
    b_out, w_up_bf = _attn_group(u4, w_up)
    x1, h2 = _out_proj(x2, y_slabs.reshape(CONV_LB, b * s, LANES),
                       b_out.reshape(b * s, ATTN_WIDTH), w_out_bf,
                       conv_ln_g, conv_ln_b, norm2_g)
    out = _conv_ffn(x1, h2, w_up_bf, ffn_dw_w, ffn_dw_b, w_down_bf, s)
    return out.reshape(b, s, d)
```

```python
import functools

import jax
import jax.numpy as jnp
from jax import lax
from jax.experimental import pallas as pl
from jax.experimental.pallas import tpu as pltpu

D_MODEL = 2048
CONV_CH = 1024
ATTN_WIDTH = 1024
HEAD_DIM = 128
N_HEADS = ATTN_WIDTH // HEAD_DIM
CONV_WIDTH = 31
CONV_HALF = (CONV_WIDTH - 1) // 2
FFN_DIM = 5632
RMS_EPS = 1e-6
LN_EPS = 1e-5
NEG_BIG = -1e30
BAND_R = 64
DILATIONS = (1, 4, 16)
LOG2E = 1.4426950408889634

LANES = 128
BF16_ROWS = 16
VMEM_LIMIT = 56 * 1024 * 1024

F32 = jnp.float32
BF16 = jnp.bfloat16


def _cparams(sem):
    return pltpu.CompilerParams(dimension_semantics=sem, vmem_limit_bytes=VMEM_LIMIT)


IN_TM = 1024
IN_TN = 1024
IN_XPARTS = 4
assert CONV_CH == IN_TN and ATTN_WIDTH == IN_TN
STEP_VAL, STEP_GATE, STEP_Q, STEP_K, STEP_V = range(5)
SLABS_PER_STEP = IN_TN // LANES
U_SLABS = 4 * SLABS_PER_STEP
Q_SLAB0 = SLABS_PER_STEP


def _in_proj_kernel(*refs):
    x_parts = refs[:IN_XPARTS]
    g_ref, w_ref, qg_ref, kg_ref, wof_ref, o_ref, wob_ref, h_scr = refs[IN_XPARTS:]
    j = pl.program_id(1)
    rows = IN_TM // IN_XPARTS

    wob_ref[...] = wof_ref[...].astype(BF16)

    @pl.when(j == 0)
    def _():
        for p, x_ref in enumerate(x_parts):
            xf = x_ref[...]
            ms = jnp.mean(xf * xf, axis=-1, keepdims=True)
            h_scr[p * rows:(p + 1) * rows, :] = (
                xf * lax.rsqrt(ms + RMS_EPS) * g_ref[...]).astype(BF16)

    def proj():
        return jnp.dot(h_scr[...], w_ref[...], preferred_element_type=F32)

    def store_slabs(res, gain=None):
        for sb in range(SLABS_PER_STEP):
            t = res[:, sb * LANES:(sb + 1) * LANES]
            if gain is not None:
                ms = jnp.mean(t * t, axis=-1, keepdims=True)
                t = t * lax.rsqrt(ms + RMS_EPS) * gain
            o_ref[sb] = t

    @pl.when((j == STEP_VAL) | (j == STEP_V))
    def _():
        store_slabs(proj())

    @pl.when(j == STEP_GATE)
    def _():
        gate = jax.nn.sigmoid(proj())
        for sb in range(SLABS_PER_STEP):
            o_ref[sb] = o_ref[sb] * gate[:, sb * LANES:(sb + 1) * LANES]

    @pl.when(j == STEP_Q)
    def _():
        store_slabs(proj(), qg_ref[...] * (HEAD_DIM ** -0.5 * LOG2E))

    @pl.when(j == STEP_K)
    def _():
        store_slabs(proj(), kg_ref[...])


def _in_proj(x2, g, w_bf, q_norm_g, k_norm_g, w_cast):
    m, d = x2.shape
    tm, tn = IN_TM, IN_TN
    gain = pl.BlockSpec((1, HEAD_DIM), lambda i, j: (0, 0))
    n_tiles = m // tm
    n_steps = w_bf.shape[1] // tn
    assert n_steps > IN_XPARTS
    cast_rows, cast_cols = w_cast.shape[0] // n_tiles, w_cast.shape[1] // (n_steps - 1)
    assert (cast_rows * n_tiles, cast_cols * (n_steps - 1)) == w_cast.shape
    assert cast_rows % BF16_ROWS == 0 and cast_cols % LANES == 0
    cast_spec = pl.BlockSpec((cast_rows, cast_cols),
                             lambda i, j: (i, jnp.minimum(j, n_steps - 2)))

    def x_part(p):
        def index(i, j):
            nxt = (j + n_steps - 2 - p) // (n_steps - 1)
            return (IN_XPARTS * jnp.minimum(i + nxt, n_tiles - 1) + p, 0)
        return pl.BlockSpec((tm // IN_XPARTS, d), index)

    return pl.pallas_call(
        _in_proj_kernel,
        grid=(n_tiles, n_steps),
        in_specs=[x_part(p) for p in range(IN_XPARTS)] + [
            pl.BlockSpec((1, d), lambda i, j: (0, 0)),
            pl.BlockSpec((d, tn), lambda i, j: (0, j)),
            gain, gain, cast_spec,
        ],
        out_specs=[pl.BlockSpec((SLABS_PER_STEP, tm, LANES),
                                lambda i, j: (jnp.maximum(j - 1, 0), i, 0)),
                   cast_spec],
        out_shape=[jax.ShapeDtypeStruct((U_SLABS, m, LANES), F32),
                   jax.ShapeDtypeStruct(w_cast.shape, BF16)],
        scratch_shapes=[pltpu.VMEM((tm, d), BF16)],
        compiler_params=_cparams(("parallel", "arbitrary")),
        name="in_proj",
    )(*([x2] * IN_XPARTS), g.reshape(1, d), w_bf,
      q_norm_g.reshape(1, HEAD_DIM), k_norm_g.reshape(1, HEAD_DIM), w_cast)


CONV_TT = 1024
CONV_HALO = 16
CONV_RC = 64
CONV_LB = CONV_CH // LANES


def _cast_spec(w, grid):
    rows, cols = w.shape[0] // grid[0], w.shape[1] // grid[1]
    assert (rows * grid[0], cols * grid[1]) == w.shape
    assert rows % BF16_ROWS == 0 and cols % LANES == 0
    return pl.BlockSpec((rows, cols), lambda i, j: (i, j))


def _conv_kernel(a_ref, pa_ref, na_ref, w_ref, b_ref, wf_ref, o_ref, wb_ref, a_scr):
    ti = pl.program_id(1)
    nt = pl.num_programs(1)
    tt = CONV_TT

    wb_ref[...] = wf_ref[...].astype(BF16)

    keep_prev = (ti > 0).astype(F32)
    keep_next = (ti < nt - 1).astype(F32)
    for lb in range(CONV_LB):
        a_scr[lb, 0:CONV_HALO, :] = pa_ref[lb, 0] * keep_prev
        a_scr[lb, CONV_HALO:CONV_HALO + tt, :] = a_ref[lb, 0]
        a_scr[lb, CONV_HALO + tt:, :] = na_ref[lb, 0] * keep_next

    def lane_block(lb, carry):
        for rc in range(tt // CONV_RC):
            r0 = rc * CONV_RC + CONV_HALO - CONV_HALF
            acc = jnp.broadcast_to(b_ref[lb], (CONV_RC, LANES))
            for k in range(CONV_WIDTH):
                acc = acc + a_scr[lb, r0 + k:r0 + k + CONV_RC, :] * w_ref[lb, k:k + 1, :]
            o_ref[lb, 0, rc * CONV_RC:(rc + 1) * CONV_RC, :] = acc
        return carry

    lax.fori_loop(0, CONV_LB, lane_block, 0)


def _conv_group(u4, conv_dw_w, conv_dw_b, w_cast):
    _, b, s, _ = u4.shape
    tt, halo = CONV_TT, CONV_HALO
    nt = s // tt
    hb = tt // halo
    n_hblk = s // halo
    w3 = conv_dw_w.reshape(CONV_WIDTH, CONV_LB, LANES).transpose(1, 0, 2)
    b3 = conv_dw_b.reshape(CONV_LB, 1, LANES)
    full = lambda shape: pl.BlockSpec(shape, lambda bi, ti: (0,) * len(shape))
    cast_spec = _cast_spec(w_cast, (b, nt))
    return pl.pallas_call(
        _conv_kernel,
        grid=(b, nt),
        in_specs=[
            pl.BlockSpec((CONV_LB, 1, tt, LANES), lambda bi, ti: (0, bi, ti, 0)),
            pl.BlockSpec((CONV_LB, 1, halo, LANES),
                         lambda bi, ti: (0, bi, jnp.maximum(ti * hb - 1, 0), 0)),
            pl.BlockSpec((CONV_LB, 1, halo, LANES),
                         lambda bi, ti: (0, bi, jnp.minimum((ti + 1) * hb, n_hblk - 1), 0)),
            full((CONV_LB, CONV_WIDTH, LANES)), full((CONV_LB, 1, LANES)), cast_spec],
        out_specs=[pl.BlockSpec((CONV_LB, 1, tt, LANES), lambda bi, ti: (0, bi, ti, 0)),
                   cast_spec],
        out_shape=[jax.ShapeDtypeStruct((CONV_LB, b, s, LANES), F32),
                   jax.ShapeDtypeStruct(w_cast.shape, BF16)],
        scratch_shapes=[pltpu.VMEM((CONV_LB, tt + 2 * halo, LANES), F32)],
        compiler_params=_cparams(("parallel", "arbitrary")),
        name="conv_group",
    )(u4, u4, u4, w3, b3, w_cast)


ATT_QB = 128
ATT_KB = 256
ATT_NE = (ATT_KB - ATT_QB) // BAND_R + 1
ATT_C4 = 4


def _attn_kernel(slope_ref, q_ref, k_ref, v_ref, wf_ref, o_ref, wb_ref,
                 q4, k4, v4, bias_scr, acc_scr, m_scr, l_scr, out_scr):
    h = pl.program_id(1)
    s_len = q_ref.shape[0]
    cl = s_len // ATT_C4
    slope = slope_ref[h] * LOG2E

    wb_ref[...] = wf_ref[...].astype(BF16)

    for c4 in range(ATT_C4):
        dst = pl.ds(c4 * cl, cl)
        src = pl.ds(c4, cl, stride=ATT_C4)
        q4[dst, :] = q_ref[src, :]
        k4[dst, :] = k_ref[src, :]
        v4[dst, :] = v_ref[src, :]

    rows = lax.broadcasted_iota(jnp.int32, (ATT_QB, ATT_KB), 0)
    cols = lax.broadcasted_iota(jnp.int32, (ATT_QB, ATT_KB), 1)
    for w, dil in enumerate(DILATIONS):
        for e in range(ATT_NE):
            off = jnp.abs(cols - rows - e * BAND_R)
            bias = jnp.where(off <= BAND_R, -(slope * dil) * off.astype(F32), NEG_BIG)
            bias_scr[ATT_NE * w + e] = bias

    ones_rhs = jnp.ones((ATT_KB, LANES), BF16)

    def block(w, qb, kb, vb, out_idx, bias):
        qb, kb, vb = qb.astype(BF16), kb.astype(BF16), vb.astype(BF16)
        sc = lax.dot_general(qb, kb, (((1,), (1,)), ((), ())), preferred_element_type=F32)
        sc = sc + bias
        m = jnp.max(sc, axis=-1, keepdims=True)
        p = jnp.exp2(sc - m).astype(BF16)
        acc = jnp.dot(p, jnp.concatenate([vb, ones_rhs[:vb.shape[0]]], axis=1),
                      preferred_element_type=F32)
        acc_scr[w, out_idx, :] = acc[:, :HEAD_DIM]
        l_scr[w, out_idx, :] = acc[:, HEAD_DIM:]
        m_scr[w, out_idx, :] = jnp.broadcast_to(m, (ATT_QB, LANES))

    def key_start(q0, class_len, nk):
        return min(max(q0 - BAND_R, 0), class_len - nk)

    for i in range(s_len // ATT_QB):
        q0 = i * ATT_QB
        k0 = key_start(q0, s_len, ATT_KB)
        qi, ki = pl.ds(q0, ATT_QB), pl.ds(k0, ATT_KB)
        block(0, q_ref[qi, :], k_ref[ki, :], v_ref[ki, :], qi,
              bias_scr[(q0 - k0) // BAND_R])

    for c4 in range(ATT_C4):
        for i in range(cl // ATT_QB):
            q0 = i * ATT_QB
            k0 = key_start(q0, cl, ATT_KB)
            qi, ki = pl.ds(c4 * cl + q0, ATT_QB), pl.ds(c4 * cl + k0, ATT_KB)
            block(1, q4[qi, :], k4[ki, :], v4[ki, :], qi,
                  bias_scr[ATT_NE + (q0 - k0) // BAND_R])

    sub_len = s_len // DILATIONS[2]
    step4 = DILATIONS[2] // ATT_C4
    for c4 in range(ATT_C4):
        for c in range(step4):
            ki = pl.ds(c4 * cl + c, sub_len, stride=step4)
            for q0 in range(0, sub_len, ATT_QB):
                qi = pl.ds(c4 * cl + c + step4 * q0, ATT_QB, stride=step4)
                block(2, q4[qi, :], k4[ki, :], v4[ki, :], qi,
                      bias_scr[2 * ATT_NE + q0 // BAND_R][:, :sub_len])

    chunk = 256
    for c4 in range(ATT_C4):
        for r in range(cl // chunk):
            nat = pl.ds(c4 + ATT_C4 * r * chunk, chunk, stride=ATT_C4)
            grp = pl.ds(c4 * cl + r * chunk, chunk)
            m0, m1, m2 = m_scr[0, nat, :], m_scr[1, grp, :], m_scr[2, grp, :]
            mm = jnp.maximum(jnp.maximum(m0, m1), m2)
            a0, a1, a2 = jnp.exp2(m0 - mm), jnp.exp2(m1 - mm), jnp.exp2(m2 - mm)
            num = a0 * acc_scr[0, nat, :] + a1 * acc_scr[1, grp, :] + a2 * acc_scr[2, grp, :]
            den = a0 * l_scr[0, nat, :] + a1 * l_scr[1, grp, :] + a2 * l_scr[2, grp, :]
            out_scr[nat, :] = num / den
    o_ref[0] = out_scr[...].astype(o_ref.dtype)


def _attn_group(u4, w_cast):
    _, b, s, _ = u4.shape
    slopes = jnp.asarray([2.0 ** (-8.0 * (i + 1) / N_HEADS) for i in range(N_HEADS)], F32)
    head = lambda base: pl.BlockSpec((None, None, s, HEAD_DIM),
                                     lambda bi, hi: (base + hi, bi, 0, 0))
    cast_spec = _cast_spec(w_cast, (b, N_HEADS))
    return pl.pallas_call(
        _attn_kernel,
        grid=(b, N_HEADS),
        in_specs=[pl.BlockSpec(memory_space=pltpu.SMEM),
                  head(Q_SLAB0), head(Q_SLAB0 + N_HEADS), head(Q_SLAB0 + 2 * N_HEADS),
                  cast_spec],
        out_specs=[pl.BlockSpec((1, s, HEAD_DIM), lambda bi, hi: (bi, 0, hi)), cast_spec],
        out_shape=[jax.ShapeDtypeStruct((b, s, ATTN_WIDTH), BF16),
                   jax.ShapeDtypeStruct(w_cast.shape, BF16)],
        scratch_shapes=[pltpu.VMEM((s, HEAD_DIM), F32),
                        pltpu.VMEM((s, HEAD_DIM), F32),
                        pltpu.VMEM((s, HEAD_DIM), F32),
                        pltpu.VMEM((len(DILATIONS) * ATT_NE, ATT_QB, ATT_KB), F32),
                        pltpu.VMEM((3, s, HEAD_DIM), F32),
                        pltpu.VMEM((3, s, LANES), F32),
                        pltpu.VMEM((3, s, LANES), F32),
                        pltpu.VMEM((s, HEAD_DIM), F32)],
        compiler_params=_cparams(("parallel", "arbitrary")),
        name="attn_group",
    )(slopes, u4, u4, u4, w_cast)


def _out_proj_kernel(x_ref, y_ref, b_ref, w_ref, lg_ref, lb_ref, g_ref, o_ref, h_ref):
    acc = jnp.dot(b_ref[...], w_ref[CONV_CH:, :], preferred_element_type=F32)

    ys = [y_ref[lb] for lb in range(CONV_LB)]
    tot = ys[0]
    for t in ys[1:]:
        tot = tot + t
    mu = jnp.sum(tot, axis=-1, keepdims=True) * (1.0 / CONV_CH)
    sq = None
    for t in ys:
        c = t - mu
        sq = c * c if sq is None else sq + c * c
    var = jnp.sum(sq, axis=-1, keepdims=True) * (1.0 / CONV_CH)
    rstd = lax.rsqrt(var + LN_EPS)
    a_cols = []
    for lb, t in enumerate(ys):
        ls = slice(lb * LANES, (lb + 1) * LANES)
        z = (t - mu) * rstd * lg_ref[:, ls] + lb_ref[:, ls]
        a_cols.append((z * jax.nn.sigmoid(z)).astype(BF16))
    a = jnp.concatenate(a_cols, axis=1)

    acc = acc + jnp.dot(a, w_ref[0:CONV_CH, :], preferred_element_type=F32)
    x1 = x_ref[...] + acc
    o_ref[...] = x1
    ms = jnp.mean(x1 * x1, axis=-1, keepdims=True)
    h_ref[...] = (x1 * lax.rsqrt(ms + RMS_EPS) * g_ref[...]).astype(BF16)


def _out_proj(x2, y_slabs, b2, w_bf, conv_ln_g, conv_ln_b, norm2_g, tm=512):
    m, d = x2.shape
    row = lambda shape: pl.BlockSpec(shape, lambda i: (0,) * len(shape))
    return pl.pallas_call(
        _out_proj_kernel,
        grid=(m // tm,),
        in_specs=[
            pl.BlockSpec((tm, d), lambda i: (i, 0)),
            pl.BlockSpec((CONV_LB, tm, LANES), lambda i: (0, i, 0)),
            pl.BlockSpec((tm, ATTN_WIDTH), lambda i: (i, 0)),
            row((d, d)), row((1, CONV_CH)), row((1, CONV_CH)), row((1, d)),
        ],
        out_specs=[pl.BlockSpec((tm, d), lambda i: (i, 0)),
                   pl.BlockSpec((tm, d), lambda i: (i, 0))],
        out_shape=[jax.ShapeDtypeStruct((m, d), F32),
                   jax.ShapeDtypeStruct((m, d), BF16)],
        compiler_params=_cparams(("parallel",)),
        name="out_proj",
    )(x2, y_slabs, b2, w_bf, conv_ln_g.reshape(1, CONV_CH), conv_ln_b.reshape(1, CONV_CH),
      norm2_g.reshape(1, d))


FFN_TM = 1024
FFN_TF = 512
FFN_PAD = BF16_ROWS
FFN_XC = 256
FFN_NXC = D_MODEL // FFN_XC


def _ffn_kernel(h_ref, hp_ref, hn_ref, x_ref, wg_ref, wv_ref, cwg_ref, cwv_ref,
                cbg_ref, cbv_ref, wd_ref, o_ref, h_scr, ug_scr, uv_scr, *, tiles_per_seq):
    i = pl.program_id(0)
    j = pl.program_id(1)
    tm = FFN_TM

    @pl.when(j == 0)
    def _():
        keep_prev = ((i % tiles_per_seq) != 0).astype(F32)
        keep_next = ((i % tiles_per_seq) != tiles_per_seq - 1).astype(F32)
        rid = lax.broadcasted_iota(jnp.int32, (FFN_PAD, 1), 0)
        hp = hp_ref[...].astype(F32)[FFN_PAD - 1:FFN_PAD, :] * keep_prev
        hn = hn_ref[...].astype(F32)[0:1, :] * keep_next
        h_scr[0:FFN_PAD, :] = jnp.where(rid == FFN_PAD - 1, hp, 0.0).astype(BF16)
        h_scr[FFN_PAD:FFN_PAD + tm, :] = h_ref[...]
        h_scr[FFN_PAD + tm:, :] = jnp.where(rid == 0, hn, 0.0).astype(BF16)
        o_ref[...] = jnp.zeros_like(o_ref)

    for c in range(FFN_NXC):
        @pl.when(j == c)
        def _(c=c):
            o_ref[:, c * FFN_XC:(c + 1) * FFN_XC] += x_ref[...]

    hh = h_scr[...]

    def up_conv(w_ref, cw_ref, cb_ref, u_scr):
        u = jnp.dot(hh, w_ref[...], preferred_element_type=F32)
        cols = []
        for c in range(FFN_TF // LANES):
            ls = slice(c * LANES, (c + 1) * LANES)
            u_scr[c] = u[:, ls]
            cols.append(cw_ref[0:1, ls] * u_scr[c, FFN_PAD - 1:FFN_PAD - 1 + tm, :]
                        + cw_ref[1:2, ls] * u_scr[c, FFN_PAD:FFN_PAD + tm, :]
                        + cw_ref[2:3, ls] * u_scr[c, FFN_PAD + 1:FFN_PAD + 1 + tm, :]
                        + cb_ref[:, ls])
        return jnp.concatenate(cols, axis=1)

    gte = up_conv(wg_ref, cwg_ref, cbg_ref, ug_scr)
    val = up_conv(wv_ref, cwv_ref, cbv_ref, uv_scr)
    act = (gte * jax.nn.sigmoid(gte) * val).astype(BF16)
    o_ref[...] += jnp.dot(act, wd_ref[...], preferred_element_type=F32)


def _conv_ffn(x1, h2, w_up_bf, ffn_dw_w, ffn_dw_b, w_down_bf, seq_len):
    m, d = x1.shape
    tm, tf = FFN_TM, FFN_TF
    nf = FFN_DIM // tf
    assert nf >= FFN_NXC
    hb = tm // FFN_PAD
    n_hblk = m // FFN_PAD
    kern = functools.partial(_ffn_kernel, tiles_per_seq=seq_len // tm)
    return pl.pallas_call(
        kern,
        grid=(m // tm, nf),
        in_specs=[
            pl.BlockSpec((tm, d), lambda i, j: (i, 0)),
            pl.BlockSpec((FFN_PAD, d), lambda i, j: (jnp.maximum(i * hb - 1, 0), 0)),
            pl.BlockSpec((FFN_PAD, d), lambda i, j: (jnp.minimum((i + 1) * hb, n_hblk - 1), 0)),
            pl.BlockSpec((tm, FFN_XC), lambda i, j: (i, jnp.minimum(j, FFN_NXC - 1))),
            pl.BlockSpec((d, tf), lambda i, j: (0, j)),
            pl.BlockSpec((d, tf), lambda i, j: (0, nf + j)),
            pl.BlockSpec((3, tf), lambda i, j: (0, j)),
            pl.BlockSpec((3, tf), lambda i, j: (0, nf + j)),
            pl.BlockSpec((1, tf), lambda i, j: (0, j)),
            pl.BlockSpec((1, tf), lambda i, j: (0, nf + j)),
            pl.BlockSpec((tf, d), lambda i, j: (j, 0)),
        ],
        out_specs=pl.BlockSpec((tm, d), lambda i, j: (i, 0)),
        out_shape=jax.ShapeDtypeStruct((m, d), F32),
        scratch_shapes=[pltpu.VMEM((tm + 2 * FFN_PAD, d), BF16),
                        pltpu.VMEM((tf // LANES, tm + 2 * FFN_PAD, LANES), F32),
                        pltpu.VMEM((tf // LANES, tm + 2 * FFN_PAD, LANES), F32)],
        compiler_params=_cparams(("parallel", "arbitrary")),
        name="conv_ffn",
    )(h2, h2, h2, x1, w_up_bf, w_up_bf, ffn_dw_w, ffn_dw_w,
      ffn_dw_b.reshape(1, -1), ffn_dw_b.reshape(1, -1), w_down_bf)


def kernel(x, norm1_g, w_in, conv_dw_w, conv_dw_b, conv_ln_g, conv_ln_b, q_norm_g, k_norm_g,
           w_out, norm2_g, w_up, ffn_dw_w, ffn_dw_b, w_down):
    b, s, d = x.shape
    x2 = x.reshape(b * s, d)
    u, w_out_bf = _in_proj(x2, norm1_g, w_in.astype(BF16), q_norm_g, k_norm_g, w_out)
    u4 = u.reshape(U_SLABS, b, s, LANES)
    y_slabs, w_down_bf = _conv_group(u4, conv_dw_w, conv_dw_b, w_down)
    b_out, w_up_bf = _attn_group(u4, w_up)
    x1, h2 = _out_proj(x2, y_slabs.reshape(CONV_LB, b * s, LANES),
                       b_out.reshape(b * s, ATTN_WIDTH), w_out_bf,
                       conv_ln_g, conv_ln_b, norm2_g)
    out = _conv_ffn(x1, h2, w_up_bf, ffn_dw_w, ffn_dw_b, w_down_bf, s)
    return out.reshape(b, s, d)
```

```python
import functools

import jax
import jax.numpy as jnp
from jax import lax
from jax.experimental import pallas as pl
from jax.experimental.pallas import tpu as pltpu

D_MODEL = 2048
CONV_CH = 1024
ATTN_WIDTH = 1024
HEAD_DIM = 128
N_HEADS = ATTN_WIDTH // HEAD_DIM
CONV_WIDTH = 31
CONV_HALF = (CONV_WIDTH - 1) // 2
FFN_DIM = 5632
RMS_EPS = 1e-6
LN_EPS = 1e-5
NEG_BIG = -1e30
BAND_R = 64
DILATIONS = (1, 4, 16)
LOG2E = 1.4426950408889634

LANES = 128
BF16_ROWS = 16
VMEM_LIMIT = 56 * 1024 * 1024

F32 = jnp.float32
BF16 = jnp.bfloat16


def _cparams(sem):
    return pltpu.CompilerParams(dimension_semantics=sem, vmem_limit_bytes=VMEM_LIMIT)


IN_TM = 1024
IN_TN = 1024
IN_XPARTS = 4
assert CONV_CH == IN_TN and ATTN_WIDTH == IN_TN
STEP_VAL, STEP_GATE, STEP_Q, STEP_K, STEP_V = range(5)
SLABS_PER_STEP = IN_TN // LANES
U_SLABS = 4 * SLABS_PER_STEP
Q_SLAB0 = SLABS_PER_STEP


def _in_proj_kernel(*refs):
    x_parts = refs[:IN_XPARTS]
    g_ref, w_ref, qg_ref, kg_ref, wof_ref, o_ref, wob_ref, h_scr = refs[IN_XPARTS:]
    j = pl.program_id(1)
    rows = IN_TM // IN_XPARTS

    wob_ref[...] = wof_ref[...].astype(BF16)

    @pl.when(j == 0)
    def _():
        for p, x_ref in enumerate(x_parts):
            xf = x_ref[...]
            ms = jnp.mean(xf * xf, axis=-1, keepdims=True)
            h_scr[p * rows:(p + 1) * rows, :] = (
                xf * lax.rsqrt(ms + RMS_EPS) * g_ref[...]).astype(BF16)

    def proj():
        return jnp.dot(h_scr[...], w_ref[...], preferred_element_type=F32)

    def store_slabs(res, gain=None):
        for sb in range(SLABS_PER_STEP):
            t = res[:, sb * LANES:(sb + 1) * LANES]
            if gain is not None:
                ms = jnp.mean(t * t, axis=-1, keepdims=True)
                t = t * lax.rsqrt(ms + RMS_EPS) * gain
            o_ref[sb] = t

    @pl.when((j == STEP_VAL) | (j == STEP_V))
    def _():
        store_slabs(proj())

    @pl.when(j == STEP_GATE)
    def _():
        gate = jax.nn.sigmoid(proj())
        for sb in range(SLABS_PER_STEP):
            o_ref[sb] = o_ref[sb] * gate[:, sb * LANES:(sb + 1) * LANES]

    @pl.when(j == STEP_Q)
    def _():
        store_slabs(proj(), qg_ref[...] * (HEAD_DIM ** -0.5 * LOG2E))

    @pl.when(j == STEP_K)
    def _():
        store_slabs(proj(), kg_ref[...])


def _in_proj(x2, g, w_bf, q_norm_g, k_norm_g, w_cast):
    m, d = x2.shape
    tm, tn = IN_TM, IN_TN
    gain = pl.BlockSpec((1, HEAD_DIM), lambda i, j: (0, 0))
    n_tiles = m // tm
    n_steps = w_bf.shape[1] // tn
    assert n_steps > IN_XPARTS
    cast_rows, cast_cols = w_cast.shape[0] // n_tiles, w_cast.shape[1] // (n_steps - 1)
    assert (cast_rows * n_tiles, cast_cols * (n_steps - 1)) == w_cast.shape
    assert cast_rows % BF16_ROWS == 0 and cast_cols % LANES == 0
    cast_spec = pl.BlockSpec((cast_rows, cast_cols),
                             lambda i, j: (i, jnp.minimum(j, n_steps - 2)))

    def x_part(p):
        def index(i, j):
            nxt = (j + n_steps - 2 - p) // (n_steps - 1)
            return (IN_XPARTS * jnp.minimum(i + nxt, n_tiles - 1) + p, 0)
        return pl.BlockSpec((tm // IN_XPARTS, d), index)

    return pl.pallas_call(
        _in_proj_kernel,
        grid=(n_tiles, n_steps),
        in_specs=[x_part(p) for p in range(IN_XPARTS)] + [
            pl.BlockSpec((1, d), lambda i, j: (0, 0)),
            pl.BlockSpec((d, tn), lambda i, j: (0, j)),
            gain, gain, cast_spec,
        ],
        out_specs=[pl.BlockSpec((SLABS_PER_STEP, tm, LANES),
                                lambda i, j: (jnp.maximum(j - 1, 0), i, 0)),
                   cast_spec],
        out_shape=[jax.ShapeDtypeStruct((U_SLABS, m, LANES), F32),
                   jax.ShapeDtypeStruct(w_cast.shape, BF16)],
        scratch_shapes=[pltpu.VMEM((tm, d), BF16)],
        compiler_params=_cparams(("parallel", "arbitrary")),
        name="in_proj",
    )(*([x2] * IN_XPARTS), g.reshape(1, d), w_bf,
      q_norm_g.reshape(1, HEAD_DIM), k_norm_g.reshape(1, HEAD_DIM), w_cast)


CONV_TT = 1024
CONV_HALO = 16
CONV_RC = 64
CONV_LB = CONV_CH // LANES


def _cast_spec(w, grid):
    rows, cols = w.shape[0] // grid[0], w.shape[1] // grid[1]
    assert (rows * grid[0], cols * grid[1]) == w.shape
    assert rows % BF16_ROWS == 0 and cols % LANES == 0
    return pl.BlockSpec((rows, cols), lambda i, j: (i, j))


def _conv_kernel(a_ref, pa_ref, na_ref, w_ref, b_ref, wf_ref, o_ref, wb_ref, a_scr):
    ti = pl.program_id(1)
    nt = pl.num_programs(1)
    tt = CONV_TT

    wb_ref[...] = wf_ref[...].astype(BF16)

    keep_prev = (ti > 0).astype(F32)
    keep_next = (ti < nt - 1).astype(F32)
    for lb in range(CONV_LB):
        a_scr[lb, 0:CONV_HALO, :] = pa_ref[lb, 0] * keep_prev
        a_scr[lb, CONV_HALO:CONV_HALO + tt, :] = a_ref[lb, 0]
        a_scr[lb, CONV_HALO + tt:, :] = na_ref[lb, 0] * keep_next

    def lane_block(lb, carry):
        for rc in range(tt // CONV_RC):
            r0 = rc * CONV_RC + CONV_HALO - CONV_HALF
            acc = jnp.broadcast_to(b_ref[lb], (CONV_RC, LANES))
            for k in range(CONV_WIDTH):
                acc = acc + a_scr[lb, r0 + k:r0 + k + CONV_RC, :] * w_ref[lb, k:k + 1, :]
            o_ref[lb, 0, rc * CONV_RC:(rc + 1) * CONV_RC, :] = acc
        return carry

    lax.fori_loop(0, CONV_LB, lane_block, 0)


def _conv_group(u4, conv_dw_w, conv_dw_b, w_cast):
    _, b, s, _ = u4.shape
    tt, halo = CONV_TT, CONV_HALO
    nt = s // tt
    hb = tt // halo
    n_hblk = s // halo
    w3 = conv_dw_w.reshape(CONV_WIDTH, CONV_LB, LANES).transpose(1, 0, 2)
    b3 = conv_dw_b.reshape(CONV_LB, 1, LANES)
    full = lambda shape: pl.BlockSpec(shape, lambda bi, ti: (0,) * len(shape))
    cast_spec = _cast_spec(w_cast, (b, nt))
    return pl.pallas_call(
        _conv_kernel,
        grid=(b, nt),
        in_specs=[
            pl.BlockSpec((CONV_LB, 1, tt, LANES), lambda bi, ti: (0, bi, ti, 0)),
            pl.BlockSpec((CONV_LB, 1, halo, LANES),
                         lambda bi, ti: (0, bi, jnp.maximum(ti * hb - 1, 0), 0)),
            pl.BlockSpec((CONV_LB, 1, halo, LANES),
                         lambda bi, ti: (0, bi, jnp.minimum((ti + 1) * hb, n_hblk - 1), 0)),
            full((CONV_LB, CONV_WIDTH, LANES)), full((CONV_LB, 1, LANES)), cast_spec],
        out_specs=[pl.BlockSpec((CONV_LB, 1, tt, LANES), lambda bi, ti: (0, bi, ti, 0)),
                   cast_spec],
        out_shape=[jax.ShapeDtypeStruct((CONV_LB, b, s, LANES), F32),
                   jax.ShapeDtypeStruct(w_cast.shape, BF16)],
        scratch_shapes=[pltpu.VMEM((CONV_LB, tt + 2 * halo, LANES), F32)],
        compiler_params=_cparams(("parallel", "arbitrary")),
        name="conv_group",
    )(u4, u4, u4, w3, b3, w_cast)


ATT_QB = 128
ATT_KB = 256
ATT_NE = (ATT_KB - ATT_QB) // BAND_R + 1
ATT_C4 = 4


def _attn_kernel(slope_ref, q_ref, k_ref, v_ref, wf_ref, o_ref, wb_ref,
                 q4, k4, v4, bias_scr, acc_scr, m_scr, l_scr, out_scr):
    h = pl.program_id(1)
    s_len = q_ref.shape[0]
    cl = s_len // ATT_C4
    slope = slope_ref[h] * LOG2E

    wb_ref[...] = wf_ref[...].astype(BF16)

    for c4 in range(ATT_C4):
        dst = pl.ds(c4 * cl, cl)
        src = pl.ds(c4, cl, stride=ATT_C4)
        q4[dst, :] = q_ref[src, :]
        k4[dst, :] = k_ref[src, :]
        v4[dst, :] = v_ref[src, :]

    rows = lax.broadcasted_iota(jnp.int32, (ATT_QB, ATT_KB), 0)
    cols = lax.broadcasted_iota(jnp.int32, (ATT_QB, ATT_KB), 1)
    for w, dil in enumerate(DILATIONS):
        for e in range(ATT_NE):
            off = jnp.abs(cols - rows - e * BAND_R)
            bias = jnp.where(off <= BAND_R, -(slope * dil) * off.astype(F32), NEG_BIG)
            bias_scr[ATT_NE * w + e] = bias

    ones_rhs = jnp.ones((ATT_KB, LANES), BF16)

    def block(w, qb, kb, vb, out_idx, bias):
        qb, kb, vb = qb.astype(BF16), kb.astype(BF16), vb.astype(BF16)
        sc = lax.dot_general(qb, kb, (((1,), (1,)), ((), ())), preferred_element_type=F32)
        sc = sc + bias
        m = jnp.max(sc, axis=-1, keepdims=True)
        p = jnp.exp2(sc - m).astype(BF16)
        acc = jnp.dot(p, jnp.concatenate([vb, ones_rhs[:vb.shape[0]]], axis=1),
                      preferred_element_type=F32)
        acc_scr[w, out_idx, :] = acc[:, :HEAD_DIM]
        l_scr[w, out_idx, :] = acc[:, HEAD_DIM:]
        m_scr[w, out_idx, :] = jnp.broadcast_to(m, (ATT_QB, LANES))

    def key_start(q0, class_len, nk):
        return min(max(q0 - BAND_R, 0), class_len - nk)

    for i in range(s_len // ATT_QB):
        q0 = i * ATT_QB
        k0 = key_start(q0, s_len, ATT_KB)
        qi, ki = pl.ds(q0, ATT_QB), pl.ds(k0, ATT_KB)
        block(0, q_ref[qi, :], k_ref[ki, :], v_ref[ki, :], qi,
              bias_scr[(q0 - k0) // BAND_R])

    for c4 in range(ATT_C4):
        for i in range(cl // ATT_QB):
            q0 = i * ATT_QB
            k0 = key_start(q0, cl, ATT_KB)
            qi, ki = pl.ds(c4 * cl + q0, ATT_QB), pl.ds(c4 * cl + k0, ATT_KB)
            block(1, q4[qi, :], k4[ki, :], v4[ki, :], qi,
                  bias_scr[ATT_NE + (q0 - k0) // BAND_R])

    sub_len = s_len // DILATIONS[2]
    step4 = DILATIONS[2] // ATT_C4
    for c4 in range(ATT_C4):
        for c in range(step4):
            ki = pl.ds(c4 * cl + c, sub_len, stride=step4)
            for q0 in range(0, sub_len, ATT_QB):
                qi = pl.ds(c4 * cl + c + step4 * q0, ATT_QB, stride=step4)
                block(2, q4[qi, :], k4[ki, :], v4[ki, :], qi,
                      bias_scr[2 * ATT_NE + q0 // BAND_R][:, :sub_len])

    chunk = 256
    for c4 in range(ATT_C4):
        for r in range(cl // chunk):
            nat = pl.ds(c4 + ATT_C4 * r * chunk, chunk, stride=ATT_C4)
            grp = pl.ds(c4 * cl + r * chunk, chunk)
            m0, m1, m2 = m_scr[0, nat, :], m_scr[1, grp, :], m_scr[2, grp, :]
            mm = jnp.maximum(jnp.maximum(m0, m1), m2)
            a0, a1, a2 = jnp.exp2(m0 - mm), jnp.exp2(m1 - mm), jnp.exp2(m2 - mm)
            num = a0 * acc_scr[0, nat, :] + a1 * acc_scr[1, grp, :] + a2 * acc_scr[2, grp, :]
            den = a0 * l_scr[0, nat, :] + a1 * l_scr[1, grp, :] + a2 * l_scr[2, grp, :]
            out_scr[nat, :] = num / den
    o_ref[0] = out_scr[...].astype(o_ref.dtype)


def _attn_group(u4, w_cast):
    _, b, s, _ = u4.shape
    slopes = jnp.asarray([2.0 ** (-8.0 * (i + 1) / N_HEADS) for i in range(N_HEADS)], F32)
    head = lambda base: pl.BlockSpec((None, None, s, HEAD_DIM),
                                     lambda bi, hi: (base + hi, bi, 0, 0))
    cast_spec = _cast_spec(w_cast, (b, N_HEADS))
    return pl.pallas_call(
        _attn_kernel,
        grid=(b, N_HEADS),
        in_specs=[pl.BlockSpec(memory_space=pltpu.SMEM),
                  head(Q_SLAB0), head(Q_SLAB0 + N_HEADS), head(Q_SLAB0 + 2 * N_HEADS),
                  cast_spec],
        out_specs=[pl.BlockSpec((1, s, HEAD_DIM), lambda bi, hi: (bi, 0, hi)), cast_spec],
        out_shape=[jax.ShapeDtypeStruct((b, s, ATTN_WIDTH), BF16),
                   jax.ShapeDtypeStruct(w_cast.shape, BF16)],
        scratch_shapes=[pltpu.VMEM((s, HEAD_DIM), F32),
                        pltpu.VMEM((s, HEAD_DIM), F32),
                        pltpu.VMEM((s, HEAD_DIM), F32),
                        pltpu.VMEM((len(DILATIONS) * ATT_NE, ATT_QB, ATT_KB), F32),
                        pltpu.VMEM((3, s, HEAD_DIM), F32),
                        pltpu.VMEM((3, s, LANES), F32),
                        pltpu.VMEM((3, s, LANES), F32),
                        pltpu.VMEM((s, HEAD_DIM), F32)],
        compiler_params=_cparams(("parallel", "arbitrary")),
        name="attn_group",
    )(slopes, u4, u4, u4, w_cast)


def _out_proj_kernel(x_ref, y_ref, b_ref, w_ref, lg_ref, lb_ref, g_ref, o_ref, h_ref):
    acc = jnp.dot(b_ref[...], w_ref[CONV_CH:, :], preferred_element_type=F32)

    ys = [y_ref[lb] for lb in range(CONV_LB)]
    tot = ys[0]
    for t in ys[1:]:
        tot = tot + t
    mu = jnp.sum(tot, axis=-1, keepdims=True) * (1.0 / CONV_CH)
    sq = None
    for t in ys:
        c = t - mu
        sq = c * c if sq is None else sq + c * c
    var = jnp.sum(sq, axis=-1, keepdims=True) * (1.0 / CONV_CH)
    rstd = lax.rsqrt(var + LN_EPS)
    a_cols = []
    for lb, t in enumerate(ys):
        ls = slice(lb * LANES, (lb + 1) * LANES)
        z = (t - mu) * rstd * lg_ref[:, ls] + lb_ref[:, ls]
        a_cols.append((z * jax.nn.sigmoid(z)).astype(BF16))
    a = jnp.concatenate(a_cols, axis=1)

    acc = acc + jnp.dot(a, w_ref[0:CONV_CH, :], preferred_element_type=F32)
    x1 = x_ref[...] + acc
    o_ref[...] = x1
    ms = jnp.mean(x1 * x1, axis=-1, keepdims=True)
    h_ref[...] = (x1 * lax.rsqrt(ms + RMS_EPS) * g_ref[...]).astype(BF16)


def _out_proj(x2, y_slabs, b2, w_bf, conv_ln_g, conv_ln_b, norm2_g, tm=512):
    m, d = x2.shape
    row = lambda shape: pl.BlockSpec(shape, lambda i: (0,) * len(shape))
    return pl.pallas_call(
        _out_proj_kernel,
        grid=(m // tm,),
        in_specs=[
            pl.BlockSpec((tm, d), lambda i: (i, 0)),
            pl.BlockSpec((CONV_LB, tm, LANES), lambda i: (0, i, 0)),
            pl.BlockSpec((tm, ATTN_WIDTH), lambda i: (i, 0)),
            row((d, d)), row((1, CONV_CH)), row((1, CONV_CH)), row((1, d)),
        ],
        out_specs=[pl.BlockSpec((tm, d), lambda i: (i, 0)),
                   pl.BlockSpec((tm, d), lambda i: (i, 0))],
        out_shape=[jax.ShapeDtypeStruct((m, d), F32),
                   jax.ShapeDtypeStruct((m, d), BF16)],
        compiler_params=_cparams(("parallel",)),
        name="out_proj",
    )(x2, y_slabs, b2, w_bf, conv_ln_g.reshape(1, CONV_CH), conv_ln_b.reshape(1, CONV_CH),
      norm2_g.reshape(1, d))


FFN_TM = 1024
FFN_TF = 512
FFN_PAD = BF16_ROWS
FFN_UOFF = 8
FFN_XC = 256
FFN_NXC = D_MODEL // FFN_XC


def _ffn_kernel(h_ref, hp_ref, hn_ref, x_ref, wg_ref, wv_ref, cwg_ref, cwv_ref,
                cbg_ref, cbv_ref, wd_ref, o_ref, h_scr, ug_scr, uv_scr, *, tiles_per_seq):
    i = pl.program_id(0)
    j = pl.program_id(1)
    tm = FFN_TM

    @pl.when(j == 0)
    def _():
        keep_prev = ((i % tiles_per_seq) != 0).astype(F32)
        keep_next = ((i % tiles_per_seq) != tiles_per_seq - 1).astype(F32)
        rid = lax.broadcasted_iota(jnp.int32, (FFN_PAD, 1), 0)
        hp = hp_ref[...].astype(F32)[FFN_PAD - 1:FFN_PAD, :] * keep_prev
        hn = hn_ref[...].astype(F32)[0:1, :] * keep_next
        h_scr[0:tm, :] = h_ref[...]
        h_scr[tm:, :] = jnp.where(rid == 0, hn, jnp.where(rid == 1, hp, 0.0)).astype(BF16)
        o_ref[...] = jnp.zeros_like(o_ref)

    for c in range(FFN_NXC):
        @pl.when(j == c)
        def _(c=c):
            o_ref[:, c * FFN_XC:(c + 1) * FFN_XC] += x_ref[...]

    hh = h_scr[...]
    base = FFN_UOFF

    def up_conv(w_ref, cw_ref, cb_ref, u_scr):
        u = jnp.dot(hh, w_ref[...], preferred_element_type=F32)
        cols = []
        for c in range(FFN_TF // LANES):
            ls = slice(c * LANES, (c + 1) * LANES)
            u_scr[c, base:base + tm, :] = u[0:tm, ls]
            u_scr[c, base + tm:base + tm + 1, :] = u[tm:tm + 1, ls]
            u_scr[c, base - 1:base, :] = u[tm + 1:tm + 2, ls]
            cols.append(cw_ref[0:1, ls] * u_scr[c, base - 1:base - 1 + tm, :]
                        + cw_ref[1:2, ls] * u_scr[c, base:base + tm, :]
                        + cw_ref[2:3, ls] * u_scr[c, base + 1:base + 1 + tm, :]
                        + cb_ref[:, ls])
        return jnp.concatenate(cols, axis=1)

    gte = up_conv(wg_ref, cwg_ref, cbg_ref, ug_scr)
    val = up_conv(wv_ref, cwv_ref, cbv_ref, uv_scr)
    act = (gte * jax.nn.sigmoid(gte) * val).astype(BF16)
    o_ref[...] += jnp.dot(act, wd_ref[...], preferred_element_type=F32)


def _conv_ffn(x1, h2, w_up_bf, ffn_dw_w, ffn_dw_b, w_down_bf, seq_len):
    m, d = x1.shape
    tm, tf = FFN_TM, FFN_TF
    nf = FFN_DIM // tf
    assert nf >= FFN_NXC
    hb = tm // FFN_PAD
    n_hblk = m // FFN_PAD
    kern = functools.partial(_ffn_kernel, tiles_per_seq=seq_len // tm)
    return pl.pallas_call(
        kern,
        grid=(m // tm, nf),
        in_specs=[
            pl.BlockSpec((tm, d), lambda i, j: (i, 0)),
            pl.BlockSpec((FFN_PAD, d), lambda i, j: (jnp.maximum(i * hb - 1, 0), 0)),
            pl.BlockSpec((FFN_PAD, d), lambda i, j: (jnp.minimum((i + 1) * hb, n_hblk - 1), 0)),
            pl.BlockSpec((tm, FFN_XC), lambda i, j: (i, jnp.minimum(j, FFN_NXC - 1))),
            pl.BlockSpec((d, tf), lambda i, j: (0, j)),
            pl.BlockSpec((d, tf), lambda i, j: (0, nf + j)),
            pl.BlockSpec((3, tf), lambda i, j: (0, j)),
            pl.BlockSpec((3, tf), lambda i, j: (0, nf + j)),
            pl.BlockSpec((1, tf), lambda i, j: (0, j)),
            pl.BlockSpec((1, tf), lambda i, j: (0, nf + j)),
            pl.BlockSpec((tf, d), lambda i, j: (j, 0)),
        ],
        out_specs=pl.BlockSpec((tm, d), lambda i, j: (i, 0)),
        out_shape=jax.ShapeDtypeStruct((m, d), F32),
        scratch_shapes=[pltpu.VMEM((tm + FFN_PAD, d), BF16),
                        pltpu.VMEM((tf // LANES, tm + 2 * FFN_UOFF, LANES), F32),
                        pltpu.VMEM((tf // LANES, tm + 2 * FFN_UOFF, LANES), F32)],
        compiler_params=_cparams(("parallel", "arbitrary")),
        name="conv_ffn",
    )(h2, h2, h2, x1, w_up_bf, w_up_bf, ffn_dw_w, ffn_dw_w,
      ffn_dw_b.reshape(1, -1), ffn_dw_b.reshape(1, -1), w_down_bf)


def kernel(x, norm1_g, w_in, conv_dw_w, conv_dw_b, conv_ln_g, conv_ln_b, q_norm_g, k_norm_g,
           w_out, norm2_g, w_up, ffn_dw_w, ffn_dw_b, w_down):
    b, s, d = x.shape
    x2 = x.reshape(b * s, d)
    u, w_out_bf = _in_proj(x2, norm1_g, w_in.astype(BF16), q_norm_g, k_norm_g, w_out)
    u4 = u.reshape(U_SLABS, b, s, LANES)
    y_slabs, w_down_bf = _conv_group(u4, conv_dw_w, conv_dw_b, w_down)
    b_out, w_up_bf = _attn_group(u4, w_up)
    x1, h2 = _out_proj(x2, y_slabs.reshape(CONV_LB, b * s, LANES),
                       b_out.reshape(b * s, ATTN_WIDTH), w_out_bf,
                       conv_ln_g, conv_ln_b, norm2_g)
    out = _conv_ffn(x1, h2, w_up_bf, ffn_dw_w, ffn_dw_b, w_down_bf, s)
    return out.reshape(b, s, d)
```

```python
import functools

import jax
import jax.numpy as jnp
from jax import lax
from jax.experimental import pallas as pl
from jax.experimental.pallas import tpu as pltpu

D_MODEL = 2048
CONV_CH = 1024
ATTN_WIDTH = 1024
HEAD_DIM = 128
N_HEADS = ATTN_WIDTH // HEAD_DIM
CONV_WIDTH = 31
CONV_HALF = (CONV_WIDTH - 1) // 2
FFN_DIM = 5632
RMS_EPS = 1e-6
LN_EPS = 1e-5
NEG_BIG = -1e30
BAND_R = 64
DILATIONS = (1, 4, 16)
LOG2E = 1.4426950408889634

LANES = 128
BF16_ROWS = 16
VMEM_LIMIT = 56 * 1024 * 1024

F32 = jnp.float32
BF16 = jnp.bfloat16


def _cparams(sem):
    return pltpu.CompilerParams(dimension_semantics=sem, vmem_limit_bytes=VMEM_LIMIT)


IN_TM = 1024
IN_TN = 1024
IN_XPARTS = 4
assert CONV_CH == IN_TN and ATTN_WIDTH == IN_TN
STEP_VAL, STEP_GATE, STEP_Q, STEP_K, STEP_V = range(5)
SLABS_PER_STEP = IN_TN // LANES
U_SLABS = 4 * SLABS_PER_STEP
Q_SLAB0 = SLABS_PER_STEP


def _in_proj_kernel(*refs):
    x_parts = refs[:IN_XPARTS]
    g_ref, w_ref, qg_ref, kg_ref, wof_ref, o_ref, wob_ref, h_scr = refs[IN_XPARTS:]
    j = pl.program_id(1)
    rows = IN_TM // IN_XPARTS

    wob_ref[...] = wof_ref[...].astype(BF16)

    @pl.when(j == 0)
    def _():
        for p, x_ref in enumerate(x_parts):
            xf = x_ref[...]
            ms = jnp.mean(xf * xf, axis=-1, keepdims=True)
            h_scr[p * rows:(p + 1) * rows, :] = (
                xf * lax.rsqrt(ms + RMS_EPS) * g_ref[...]).astype(BF16)

    def proj():
        return jnp.dot(h_scr[...], w_ref[...], preferred_element_type=F32)

    def store_slabs(res, gain=None):
        for sb in range(SLABS_PER_STEP):
            t = res[:, sb * LANES:(sb + 1) * LANES]
            if gain is not None:
                ms = jnp.mean(t * t, axis=-1, keepdims=True)
                t = t * lax.rsqrt(ms + RMS_EPS) * gain
            o_ref[sb] = t

    @pl.when((j == STEP_VAL) | (j == STEP_V))
    def _():
        store_slabs(proj())

    @pl.when(j == STEP_GATE)
    def _():
        gate = jax.nn.sigmoid(proj())
        for sb in range(SLABS_PER_STEP):
            o_ref[sb] = o_ref[sb] * gate[:, sb * LANES:(sb + 1) * LANES]

    @pl.when(j == STEP_Q)
    def _():
        store_slabs(proj(), qg_ref[...] * (HEAD_DIM ** -0.5 * LOG2E))

    @pl.when(j == STEP_K)
    def _():
        store_slabs(proj(), kg_ref[...])


def _in_proj(x2, g, w_bf, q_norm_g, k_norm_g, w_cast):
    m, d = x2.shape
    tm, tn = IN_TM, IN_TN
    gain = pl.BlockSpec((1, HEAD_DIM), lambda i, j: (0, 0))
    n_tiles = m // tm
    n_steps = w_bf.shape[1] // tn
    assert n_steps > IN_XPARTS
    cast_rows, cast_cols = w_cast.shape[0] // n_tiles, w_cast.shape[1] // (n_steps - 1)
    assert (cast_rows * n_tiles, cast_cols * (n_steps - 1)) == w_cast.shape
    assert cast_rows % BF16_ROWS == 0 and cast_cols % LANES == 0
    cast_spec = pl.BlockSpec((cast_rows, cast_cols),
                             lambda i, j: (i, jnp.minimum(j, n_steps - 2)))

    def x_part(p):
        def index(i, j):
            nxt = (j + n_steps - 2 - p) // (n_steps - 1)
            return (IN_XPARTS * jnp.minimum(i + nxt, n_tiles - 1) + p, 0)
        return pl.BlockSpec((tm // IN_XPARTS, d), index)

    return pl.pallas_call(
        _in_proj_kernel,
        grid=(n_tiles, n_steps),
        in_specs=[x_part(p) for p in range(IN_XPARTS)] + [
            pl.BlockSpec((1, d), lambda i, j: (0, 0)),
            pl.BlockSpec((d, tn), lambda i, j: (0, j)),
            gain, gain, cast_spec,
        ],
        out_specs=[pl.BlockSpec((SLABS_PER_STEP, tm, LANES),
                                lambda i, j: (jnp.maximum(j - 1, 0), i, 0)),
                   cast_spec],
        out_shape=[jax.ShapeDtypeStruct((U_SLABS, m, LANES), F32),
                   jax.ShapeDtypeStruct(w_cast.shape, BF16)],
        scratch_shapes=[pltpu.VMEM((tm, d), BF16)],
        compiler_params=_cparams(("parallel", "arbitrary")),
        name="in_proj",
    )(*([x2] * IN_XPARTS), g.reshape(1, d), w_bf,
      q_norm_g.reshape(1, HEAD_DIM), k_norm_g.reshape(1, HEAD_DIM), w_cast)


CONV_TT = 1024
CONV_HALO = 16
CONV_RC = 64
CONV_LB = CONV_CH // LANES


def _cast_spec(w, grid):
    rows, cols = w.shape[0] // grid[0], w.shape[1] // grid[1]
    assert (rows * grid[0], cols * grid[1]) == w.shape
    assert rows % BF16_ROWS == 0 and cols % LANES == 0
    return pl.BlockSpec((rows, cols), lambda i, j: (i, j))


def _conv_kernel(a_ref, pa_ref, na_ref, w_ref, b_ref, wf_ref, o_ref, wb_ref, a_scr):
    ti = pl.program_id(1)
    nt = pl.num_programs(1)
    tt = CONV_TT

    wb_ref[...] = wf_ref[...].astype(BF16)

    keep_prev = (ti > 0).astype(F32)
    keep_next = (ti < nt - 1).astype(F32)
    for lb in range(CONV_LB):
        a_scr[lb, 0:CONV_HALO, :] = pa_ref[lb, 0] * keep_prev
        a_scr[lb, CONV_HALO:CONV_HALO + tt, :] = a_ref[lb, 0]
        a_scr[lb, CONV_HALO + tt:, :] = na_ref[lb, 0] * keep_next

    def lane_block(lb, carry):
        for rc in range(tt // CONV_RC):
            r0 = rc * CONV_RC + CONV_HALO - CONV_HALF
            acc = jnp.broadcast_to(b_ref[lb], (CONV_RC, LANES))
            for k in range(CONV_WIDTH):
                acc = acc + a_scr[lb, r0 + k:r0 + k + CONV_RC, :] * w_ref[lb, k:k + 1, :]
            o_ref[lb, 0, rc * CONV_RC:(rc + 1) * CONV_RC, :] = acc
        return carry

    lax.fori_loop(0, CONV_LB, lane_block, 0)


def _conv_group(u4, conv_dw_w, conv_dw_b, w_cast):
    _, b, s, _ = u4.shape
    tt, halo = CONV_TT, CONV_HALO
    nt = s // tt
    hb = tt // halo
    n_hblk = s // halo
    w3 = conv_dw_w.reshape(CONV_WIDTH, CONV_LB, LANES).transpose(1, 0, 2)
    b3 = conv_dw_b.reshape(CONV_LB, 1, LANES)
    full = lambda shape: pl.BlockSpec(shape, lambda bi, ti: (0,) * len(shape))
    cast_spec = _cast_spec(w_cast, (b, nt))
    return pl.pallas_call(
        _conv_kernel,
        grid=(b, nt),
        in_specs=[
            pl.BlockSpec((CONV_LB, 1, tt, LANES), lambda bi, ti: (0, bi, ti, 0)),
            pl.BlockSpec((CONV_LB, 1, halo, LANES),
                         lambda bi, ti: (0, bi, jnp.maximum(ti * hb - 1, 0), 0)),
            pl.BlockSpec((CONV_LB, 1, halo, LANES),
                         lambda bi, ti: (0, bi, jnp.minimum((ti + 1) * hb, n_hblk - 1), 0)),
            full((CONV_LB, CONV_WIDTH, LANES)), full((CONV_LB, 1, LANES)), cast_spec],
        out_specs=[pl.BlockSpec((CONV_LB, 1, tt, LANES), lambda bi, ti: (0, bi, ti, 0)),
                   cast_spec],
        out_shape=[jax.ShapeDtypeStruct((CONV_LB, b, s, LANES), F32),
                   jax.ShapeDtypeStruct(w_cast.shape, BF16)],
        scratch_shapes=[pltpu.VMEM((CONV_LB, tt + 2 * halo, LANES), F32)],
        compiler_params=_cparams(("parallel", "arbitrary")),
        name="conv_group",
    )(u4, u4, u4, w3, b3, w_cast)


ATT_QB = 128
ATT_KB = 256
ATT_NE = (ATT_KB - ATT_QB) // BAND_R + 1
ATT_C4 = 4


def _attn_kernel(slope_ref, q_ref, k_ref, v_ref, wf_ref, o_ref, wb_ref,
                 q4, k4, v4, bias_scr, acc_scr, m_scr, l_scr, out_scr):
    h = pl.program_id(1)
    s_len = q_ref.shape[0]
    cl = s_len // ATT_C4
    slope = slope_ref[h] * LOG2E

    wb_ref[...] = wf_ref[...].astype(BF16)

    for c4 in range(ATT_C4):
        dst = pl.ds(c4 * cl, cl)
        src = pl.ds(c4, cl, stride=ATT_C4)
        q4[dst, :] = q_ref[src, :]
        k4[dst, :] = k_ref[src, :]
        v4[dst, :] = v_ref[src, :]

    rows = lax.broadcasted_iota(jnp.int32, (ATT_QB, ATT_KB), 0)
    cols = lax.broadcasted_iota(jnp.int32, (ATT_QB, ATT_KB), 1)
    for w, dil in enumerate(DILATIONS):
        for e in range(ATT_NE):
            off = jnp.abs(cols - rows - e * BAND_R)
            bias = jnp.where(off <= BAND_R, -(slope * dil) * off.astype(F32), NEG_BIG)
            bias_scr[ATT_NE * w + e] = bias

    ones_rhs = jnp.ones((ATT_KB, LANES), BF16)

    def block(w, qb, kb, vb, out_idx, bias):
        qb, kb, vb = qb.astype(BF16), kb.astype(BF16), vb.astype(BF16)
        sc = lax.dot_general(qb, kb, (((1,), (1,)), ((), ())), preferred_element_type=F32)
        sc = sc + bias
        m = jnp.max(sc, axis=-1, keepdims=True)
        p = jnp.exp2(sc - m).astype(BF16)
        acc = jnp.dot(p, jnp.concatenate([vb, ones_rhs[:vb.shape[0]]], axis=1),
                      preferred_element_type=F32)
        acc_scr[w, out_idx, :] = acc[:, :HEAD_DIM]
        l_scr[w, out_idx, :] = acc[:, HEAD_DIM:]
        m_scr[w, out_idx, :] = jnp.broadcast_to(m, (ATT_QB, LANES))

    def key_start(q0, class_len, nk):
        return min(max(q0 - BAND_R, 0), class_len - nk)

    for i in range(s_len // ATT_QB):
        q0 = i * ATT_QB
        k0 = key_start(q0, s_len, ATT_KB)
        qi, ki = pl.ds(q0, ATT_QB), pl.ds(k0, ATT_KB)
        block(0, q_ref[qi, :], k_ref[ki, :], v_ref[ki, :], qi,
              bias_scr[(q0 - k0) // BAND_R])

    for c4 in range(ATT_C4):
        for i in range(cl // ATT_QB):
            q0 = i * ATT_QB
            k0 = key_start(q0, cl, ATT_KB)
            qi, ki = pl.ds(c4 * cl + q0, ATT_QB), pl.ds(c4 * cl + k0, ATT_KB)
            block(1, q4[qi, :], k4[ki, :], v4[ki, :], qi,
                  bias_scr[ATT_NE + (q0 - k0) // BAND_R])

    sub_len = s_len // DILATIONS[2]
    step4 = DILATIONS[2] // ATT_C4
    for c4 in range(ATT_C4):
        for c in range(step4):
            ki = pl.ds(c4 * cl + c, sub_len, stride=step4)
            for q0 in range(0, sub_len, ATT_QB):
                qi = pl.ds(c4 * cl + c + step4 * q0, ATT_QB, stride=step4)
                block(2, q4[qi, :], k4[ki, :], v4[ki, :], qi,
                      bias_scr[2 * ATT_NE + q0 // BAND_R][:, :sub_len])

    chunk = 256
    for c4 in range(ATT_C4):
        for r in range(cl // chunk):
            nat = pl.ds(c4 + ATT_C4 * r * chunk, chunk, stride=ATT_C4)
            grp = pl.ds(c4 * cl + r * chunk, chunk)
            m0, m1, m2 = m_scr[0, nat, :], m_scr[1, grp, :], m_scr[2, grp, :]
            mm = jnp.maximum(jnp.maximum(m0, m1), m2)
            a0, a1, a2 = jnp.exp2(m0 - mm), jnp.exp2(m1 - mm), jnp.exp2(m2 - mm)
            num = a0 * acc_scr[0, nat, :] + a1 * acc_scr[1, grp, :] + a2 * acc_scr[2, grp, :]
            den = a0 * l_scr[0, nat, :] + a1 * l_scr[1, grp, :] + a2 * l_scr[2, grp, :]
            out_scr[nat, :] = num / den
    o_ref[0] = out_scr[...].astype(o_ref.dtype)


def _attn_group(u4, w_cast):
    _, b, s, _ = u4.shape
    slopes = jnp.asarray([2.0 ** (-8.0 * (i + 1) / N_HEADS) for i in range(N_HEADS)], F32)
    head = lambda base: pl.BlockSpec((None, None, s, HEAD_DIM),
                                     lambda bi, hi: (base + hi, bi, 0, 0))
    cast_spec = _cast_spec(w_cast, (b, N_HEADS))
    return pl.pallas_call(
        _attn_kernel,
        grid=(b, N_HEADS),
        in_specs=[pl.BlockSpec(memory_space=pltpu.SMEM),
                  head(Q_SLAB0), head(Q_SLAB0 + N_HEADS), head(Q_SLAB0 + 2 * N_HEADS),
                  cast_spec],
        out_specs=[pl.BlockSpec((1, s, HEAD_DIM), lambda bi, hi: (bi, 0, hi)), cast_spec],
        out_shape=[jax.ShapeDtypeStruct((b, s, ATTN_WIDTH), BF16),
                   jax.ShapeDtypeStruct(w_cast.shape, BF16)],
        scratch_shapes=[pltpu.VMEM((s, HEAD_DIM), F32),
                        pltpu.VMEM((s, HEAD_DIM), F32),
                        pltpu.VMEM((s, HEAD_DIM), F32),
                        pltpu.VMEM((len(DILATIONS) * ATT_NE, ATT_QB, ATT_KB), F32),
                        pltpu.VMEM((3, s, HEAD_DIM), F32),
                        pltpu.VMEM((3, s, LANES), F32),
                        pltpu.VMEM((3, s, LANES), F32),
                        pltpu.VMEM((s, HEAD_DIM), F32)],
        compiler_params=_cparams(("parallel", "arbitrary")),
        name="attn_group",
    )(slopes, u4, u4, u4, w_cast)


def _out_proj_kernel(x_ref, y_ref, b_ref, w_ref, lg_ref, lb_ref, g_ref, o_ref, h_ref):
    acc = jnp.dot(b_ref[...], w_ref[CONV_CH:, :], preferred_element_type=F32)

    ys = [y_ref[lb] for lb in range(CONV_LB)]
    tot = ys[0]
    for t in ys[1:]:
        tot = tot + t
    mu = jnp.sum(tot, axis=-1, keepdims=True) * (1.0 / CONV_CH)
    sq = None
    for t in ys:
        c = t - mu
        sq = c * c if sq is None else sq + c * c
    var = jnp.sum(sq, axis=-1, keepdims=True) * (1.0 / CONV_CH)
    rstd = lax.rsqrt(var + LN_EPS)
    a_cols = []
    for lb, t in enumerate(ys):
        ls = slice(lb * LANES, (lb + 1) * LANES)
        z = (t - mu) * rstd * lg_ref[:, ls] + lb_ref[:, ls]
        a_cols.append((z * jax.nn.sigmoid(z)).astype(BF16))
    a = jnp.concatenate(a_cols, axis=1)

    acc = acc + jnp.dot(a, w_ref[0:CONV_CH, :], preferred_element_type=F32)
    x1 = x_ref[...] + acc
    o_ref[...] = x1
    ms = jnp.mean(x1 * x1, axis=-1, keepdims=True)
    h_ref[...] = (x1 * lax.rsqrt(ms + RMS_EPS) * g_ref[...]).astype(BF16)


def _out_proj(x2, y_slabs, b2, w_bf, conv_ln_g, conv_ln_b, norm2_g, tm=512):
    m, d = x2.shape
    row = lambda shape: pl.BlockSpec(shape, lambda i: (0,) * len(shape))
    return pl.pallas_call(
        _out_proj_kernel,
        grid=(m // tm,),
        in_specs=[
            pl.BlockSpec((tm, d), lambda i: (i, 0)),
            pl.BlockSpec((CONV_LB, tm, LANES), lambda i: (0, i, 0)),
            pl.BlockSpec((tm, ATTN_WIDTH), lambda i: (i, 0)),
            row((d, d)), row((1, CONV_CH)), row((1, CONV_CH)), row((1, d)),
        ],
        out_specs=[pl.BlockSpec((tm, d), lambda i: (i, 0)),
                   pl.BlockSpec((tm, d), lambda i: (i, 0))],
        out_shape=[jax.ShapeDtypeStruct((m, d), F32),
                   jax.ShapeDtypeStruct((m, d), BF16)],
        compiler_params=_cparams(("parallel",)),
        name="out_proj",
    )(x2, y_slabs, b2, w_bf, conv_ln_g.reshape(1, CONV_CH), conv_ln_b.reshape(1, CONV_CH),
      norm2_g.reshape(1, d))


FFN_TM = 1024
FFN_TF = 512
FFN_PAD = BF16_ROWS
FFN_UOFF = 8
FFN_GROUP = 2
FFN_XC = 256
FFN_NXC = D_MODEL // FFN_XC


def _ffn_kernel(h_ref, hp_ref, hn_ref, x_ref, wg_ref, wv_ref, cwg_ref, cwv_ref,
                cbg_ref, cbv_ref, wd_ref, o_ref, h_scr, ug_scr, uv_scr, *, tiles_per_seq):
    i = pl.program_id(0)
    j = pl.program_id(1)
    tm = FFN_TM

    @pl.when(j == 0)
    def _():
        keep_prev = ((i % tiles_per_seq) != 0).astype(F32)
        keep_next = ((i % tiles_per_seq) != tiles_per_seq - 1).astype(F32)
        rid = lax.broadcasted_iota(jnp.int32, (FFN_PAD, 1), 0)
        hp = hp_ref[...].astype(F32)[FFN_PAD - 1:FFN_PAD, :] * keep_prev
        hn = hn_ref[...].astype(F32)[0:1, :] * keep_next
        h_scr[0:tm, :] = h_ref[...]
        h_scr[tm:, :] = jnp.where(rid == 0, hn, jnp.where(rid == 1, hp, 0.0)).astype(BF16)
        o_ref[...] = jnp.zeros_like(o_ref)

    for c in range(FFN_NXC):
        @pl.when(j == c)
        def _(c=c):
            o_ref[:, c * FFN_XC:(c + 1) * FFN_XC] += x_ref[...]

    base = FFN_UOFF

    def up_conv(w_ref, cw_ref, cb_ref, u_scr, c0, c1):
        u = jnp.dot(h_scr[...], w_ref[:, c0 * LANES:c1 * LANES], preferred_element_type=F32)
        cols = []
        for c in range(c0, c1):
            ls = slice(c * LANES, (c + 1) * LANES)
            us = slice((c - c0) * LANES, (c - c0 + 1) * LANES)
            u_scr[c, base:base + tm, :] = u[0:tm, us]
            u_scr[c, base + tm:base + tm + 1, :] = u[tm:tm + 1, us]
            u_scr[c, base - 1:base, :] = u[tm + 1:tm + 2, us]
            cols.append(cw_ref[0:1, ls] * u_scr[c, base - 1:base - 1 + tm, :]
                        + cw_ref[1:2, ls] * u_scr[c, base:base + tm, :]
                        + cw_ref[2:3, ls] * u_scr[c, base + 1:base + 1 + tm, :]
                        + cb_ref[:, ls])
        return jnp.concatenate(cols, axis=1)

    acts = []
    for c0 in range(0, FFN_TF // LANES, FFN_GROUP):
        gte = up_conv(wg_ref, cwg_ref, cbg_ref, ug_scr, c0, c0 + FFN_GROUP)
        val = up_conv(wv_ref, cwv_ref, cbv_ref, uv_scr, c0, c0 + FFN_GROUP)
        acts.append((gte * jax.nn.sigmoid(gte) * val).astype(BF16))
    act = jnp.concatenate(acts, axis=1)
    o_ref[...] += jnp.dot(act, wd_ref[...], preferred_element_type=F32)


def _conv_ffn(x1, h2, w_up_bf, ffn_dw_w, ffn_dw_b, w_down_bf, seq_len):
    m, d = x1.shape
    tm, tf = FFN_TM, FFN_TF
    nf = FFN_DIM // tf
    assert nf >= FFN_NXC
    hb = tm // FFN_PAD
    n_hblk = m // FFN_PAD
    kern = functools.partial(_ffn_kernel, tiles_per_seq=seq_len // tm)
    return pl.pallas_call(
        kern,
        grid=(m // tm, nf),
        in_specs=[
            pl.BlockSpec((tm, d), lambda i, j: (i, 0)),
            pl.BlockSpec((FFN_PAD, d), lambda i, j: (jnp.maximum(i * hb - 1, 0), 0)),
            pl.BlockSpec((FFN_PAD, d), lambda i, j: (jnp.minimum((i + 1) * hb, n_hblk - 1), 0)),
            pl.BlockSpec((tm, FFN_XC), lambda i, j: (i, jnp.minimum(j, FFN_NXC - 1))),
            pl.BlockSpec((d, tf), lambda i, j: (0, j)),
            pl.BlockSpec((d, tf), lambda i, j: (0, nf + j)),
            pl.BlockSpec((3, tf), lambda i, j: (0, j)),
            pl.BlockSpec((3, tf), lambda i, j: (0, nf + j)),
            pl.BlockSpec((1, tf), lambda i, j: (0, j)),
            pl.BlockSpec((1, tf), lambda i, j: (0, nf + j)),
            pl.BlockSpec((tf, d), lambda i, j: (j, 0)),
        ],
        out_specs=pl.BlockSpec((tm, d), lambda i, j: (i, 0)),
        out_shape=jax.ShapeDtypeStruct((m, d), F32),
        scratch_shapes=[pltpu.VMEM((tm + FFN_PAD, d), BF16),
                        pltpu.VMEM((tf // LANES, tm + 2 * FFN_UOFF, LANES), F32),
                        pltpu.VMEM((tf // LANES, tm + 2 * FFN_UOFF, LANES), F32)],
        compiler_params=_cparams(("parallel", "arbitrary")),
        name="conv_ffn",
    )(h2, h2, h2, x1, w_up_bf, w_up_bf, ffn_dw_w, ffn_dw_w,
      ffn_dw_b.reshape(1, -1), ffn_dw_b.reshape(1, -1), w_down_bf)


def kernel(x, norm1_g, w_in, conv_dw_w, conv_dw_b, conv_ln_g, conv_ln_b, q_norm_g, k_norm_g,
           w_out, norm2_g, w_up, ffn_dw_w, ffn_dw_b, w_down):
    b, s, d = x.shape
    x2 = x.reshape(b * s, d)
    u, w_out_bf = _in_proj(x2, norm1_g, w_in.astype(BF16), q_norm_g, k_norm_g, w_out)
    u4 = u.reshape(U_SLABS, b, s, LANES)
    y_slabs, w_down_bf = _conv_group(u4, conv_dw_w, conv_dw_b, w_down)
    b_out, w_up_bf = _attn_group(u4, w_up)
    x1, h2 = _out_proj(x2, y_slabs.reshape(CONV_LB, b * s, LANES),
                       b_out.reshape(b * s, ATTN_WIDTH), w_out_bf,
                       conv_ln_g, conv_ln_b, norm2_g)
    out = _conv_ffn(x1, h2, w_up_bf, ffn_dw_w, ffn_dw_b, w_down_bf, s)
    return out.reshape(b, s, d)
```

```python
import functools

import jax
import jax.numpy as jnp
from jax import lax
from jax.experimental import pallas as pl
from jax.experimental.pallas import tpu as pltpu

D_MODEL = 2048
CONV_CH = 1024
ATTN_WIDTH = 1024
HEAD_DIM = 128
N_HEADS = ATTN_WIDTH // HEAD_DIM
CONV_WIDTH = 31
CONV_HALF = (CONV_WIDTH - 1) // 2
FFN_DIM = 5632
RMS_EPS = 1e-6
LN_EPS = 1e-5
NEG_BIG = -1e30
BAND_R = 64
DILATIONS = (1, 4, 16)
LOG2E = 1.4426950408889634

LANES = 128
BF16_ROWS = 16
VMEM_LIMIT = 56 * 1024 * 1024

F32 = jnp.float32
BF16 = jnp.bfloat16


def _cparams(sem):
    return pltpu.CompilerParams(dimension_semantics=sem, vmem_limit_bytes=VMEM_LIMIT)


IN_TM = 1024
IN_TN = 1024
IN_XPARTS = 4
assert CONV_CH == IN_TN and ATTN_WIDTH == IN_TN
STEP_VAL, STEP_GATE, STEP_Q, STEP_K, STEP_V = range(5)
SLABS_PER_STEP = IN_TN // LANES
U_SLABS = 4 * SLABS_PER_STEP
Q_SLAB0 = SLABS_PER_STEP


def _in_proj_kernel(*refs):
    x_parts = refs[:IN_XPARTS]
    g_ref, w_ref, qg_ref, kg_ref, wof_ref, o_ref, wob_ref, h_scr = refs[IN_XPARTS:]
    j = pl.program_id(1)
    rows = IN_TM // IN_XPARTS

    wob_ref[...] = wof_ref[...].astype(BF16)

    @pl.when(j == 0)
    def _():
        for p, x_ref in enumerate(x_parts):
            xf = x_ref[...]
            ms = jnp.mean(xf * xf, axis=-1, keepdims=True)
            h_scr[p * rows:(p + 1) * rows, :] = (
                xf * lax.rsqrt(ms + RMS_EPS) * g_ref[...]).astype(BF16)

    def proj():
        return jnp.dot(h_scr[...], w_ref[...], preferred_element_type=F32)

    def store_slabs(res, gain=None):
        for sb in range(SLABS_PER_STEP):
            t = res[:, sb * LANES:(sb + 1) * LANES]
            if gain is not None:
                ms = jnp.mean(t * t, axis=-1, keepdims=True)
                t = t * lax.rsqrt(ms + RMS_EPS) * gain
            o_ref[sb] = t

    @pl.when((j == STEP_VAL) | (j == STEP_V))
    def _():
        store_slabs(proj())

    @pl.when(j == STEP_GATE)
    def _():
        gate = jax.nn.sigmoid(proj())
        for sb in range(SLABS_PER_STEP):
            o_ref[sb] = o_ref[sb] * gate[:, sb * LANES:(sb + 1) * LANES]

    @pl.when(j == STEP_Q)
    def _():
        store_slabs(proj(), qg_ref[...] * (HEAD_DIM ** -0.5 * LOG2E))

    @pl.when(j == STEP_K)
    def _():
        store_slabs(proj(), kg_ref[...])


def _in_proj(x2, g, w_bf, q_norm_g, k_norm_g, w_cast):
    m, d = x2.shape
    tm, tn = IN_TM, IN_TN
    gain = pl.BlockSpec((1, HEAD_DIM), lambda i, j: (0, 0))
    n_tiles = m // tm
    n_steps = w_bf.shape[1] // tn
    assert n_steps > IN_XPARTS
    cast_rows, cast_cols = w_cast.shape[0] // n_tiles, w_cast.shape[1] // (n_steps - 1)
    assert (cast_rows * n_tiles, cast_cols * (n_steps - 1)) == w_cast.shape
    assert cast_rows % BF16_ROWS == 0 and cast_cols % LANES == 0
    cast_spec = pl.BlockSpec((cast_rows, cast_cols),
                             lambda i, j: (i, jnp.minimum(j, n_steps - 2)))

    def x_part(p):
        def index(i, j):
            nxt = (j + n_steps - 2 - p) // (n_steps - 1)
            return (IN_XPARTS * jnp.minimum(i + nxt, n_tiles - 1) + p, 0)
        return pl.BlockSpec((tm // IN_XPARTS, d), index)

    return pl.pallas_call(
        _in_proj_kernel,
        grid=(n_tiles, n_steps),
        in_specs=[x_part(p) for p in range(IN_XPARTS)] + [
            pl.BlockSpec((1, d), lambda i, j: (0, 0)),
            pl.BlockSpec((d, tn), lambda i, j: (0, j)),
            gain, gain, cast_spec,
        ],
        out_specs=[pl.BlockSpec((SLABS_PER_STEP, tm, LANES),
                                lambda i, j: (jnp.maximum(j - 1, 0), i, 0)),
                   cast_spec],
        out_shape=[jax.ShapeDtypeStruct((U_SLABS, m, LANES), F32),
                   jax.ShapeDtypeStruct(w_cast.shape, BF16)],
        scratch_shapes=[pltpu.VMEM((tm, d), BF16)],
        compiler_params=_cparams(("parallel", "arbitrary")),
        name="in_proj",
    )(*([x2] * IN_XPARTS), g.reshape(1, d), w_bf,
      q_norm_g.reshape(1, HEAD_DIM), k_norm_g.reshape(1, HEAD_DIM), w_cast)


CONV_TT = 1024
CONV_HALO = 16
CONV_RC = 64
CONV_LB = CONV_CH // LANES


def _cast_spec(w, grid):
    rows, cols = w.shape[0] // grid[0], w.shape[1] // grid[1]
    assert (rows * grid[0], cols * grid[1]) == w.shape
    assert rows % BF16_ROWS == 0 and cols % LANES == 0
    return pl.BlockSpec((rows, cols), lambda i, j: (i, j))


def _conv_kernel(a_ref, pa_ref, na_ref, w_ref, b_ref, wf_ref, o_ref, wb_ref, a_scr):
    ti = pl.program_id(1)
    nt = pl.num_programs(1)
    tt = CONV_TT

    wb_ref[...] = wf_ref[...].astype(BF16)

    keep_prev = (ti > 0).astype(F32)
    keep_next = (ti < nt - 1).astype(F32)
    for lb in range(CONV_LB):
        a_scr[lb, 0:CONV_HALO, :] = pa_ref[lb, 0] * keep_prev
        a_scr[lb, CONV_HALO:CONV_HALO + tt, :] = a_ref[lb, 0]
        a_scr[lb, CONV_HALO + tt:, :] = na_ref[lb, 0] * keep_next

    def lane_block(lb, carry):
        for rc in range(tt // CONV_RC):
            r0 = rc * CONV_RC + CONV_HALO - CONV_HALF
            acc = jnp.broadcast_to(b_ref[lb], (CONV_RC, LANES))
            for k in range(CONV_WIDTH):
                acc = acc + a_scr[lb, r0 + k:r0 + k + CONV_RC, :] * w_ref[lb, k:k + 1, :]
            o_ref[lb, 0, rc * CONV_RC:(rc + 1) * CONV_RC, :] = acc
        return carry

    lax.fori_loop(0, CONV_LB, lane_block, 0)


def _conv_group(u4, conv_dw_w, conv_dw_b, w_cast):
    _, b, s, _ = u4.shape
    tt, halo = CONV_TT, CONV_HALO
    nt = s // tt
    hb = tt // halo
    n_hblk = s // halo
    w3 = conv_dw_w.reshape(CONV_WIDTH, CONV_LB, LANES).transpose(1, 0, 2)
    b3 = conv_dw_b.reshape(CONV_LB, 1, LANES)
    full = lambda shape: pl.BlockSpec(shape, lambda bi, ti: (0,) * len(shape))
    cast_spec = _cast_spec(w_cast, (b, nt))
    return pl.pallas_call(
        _conv_kernel,
        grid=(b, nt),
        in_specs=[
            pl.BlockSpec((CONV_LB, 1, tt, LANES), lambda bi, ti: (0, bi, ti, 0)),
            pl.BlockSpec((CONV_LB, 1, halo, LANES),
                         lambda bi, ti: (0, bi, jnp.maximum(ti * hb - 1, 0), 0)),
            pl.BlockSpec((CONV_LB, 1, halo, LANES),
                         lambda bi, ti: (0, bi, jnp.minimum((ti + 1) * hb, n_hblk - 1), 0)),
            full((CONV_LB, CONV_WIDTH, LANES)), full((CONV_LB, 1, LANES)), cast_spec],
        out_specs=[pl.BlockSpec((CONV_LB, 1, tt, LANES), lambda bi, ti: (0, bi, ti, 0)),
                   cast_spec],
        out_shape=[jax.ShapeDtypeStruct((CONV_LB, b, s, LANES), F32),
                   jax.ShapeDtypeStruct(w_cast.shape, BF16)],
        scratch_shapes=[pltpu.VMEM((CONV_LB, tt + 2 * halo, LANES), F32)],
        compiler_params=_cparams(("parallel", "arbitrary")),
        name="conv_group",
    )(u4, u4, u4, w3, b3, w_cast)


ATT_QB = 128
ATT_KB = 256
ATT_NE = (ATT_KB - ATT_QB) // BAND_R + 1
ATT_C4 = 4


def _attn_kernel(slope_ref, q_ref, k_ref, v_ref, wf_ref, o_ref, wb_ref,
                 q4, k4, v4, bias_scr, acc_scr, m_scr, l_scr, out_scr):
    h = pl.program_id(1)
    s_len = q_ref.shape[0]
    cl = s_len // ATT_C4
    slope = slope_ref[h] * LOG2E

    wb_ref[...] = wf_ref[...].astype(BF16)

    for c4 in range(ATT_C4):
        dst = pl.ds(c4 * cl, cl)
        src = pl.ds(c4, cl, stride=ATT_C4)
        q4[dst, :] = q_ref[src, :]
        k4[dst, :] = k_ref[src, :]
        v4[dst, :] = v_ref[src, :]

    rows = lax.broadcasted_iota(jnp.int32, (ATT_QB, ATT_KB), 0)
    cols = lax.broadcasted_iota(jnp.int32, (ATT_QB, ATT_KB), 1)
    for w, dil in enumerate(DILATIONS):
        for e in range(ATT_NE):
            off = jnp.abs(cols - rows - e * BAND_R)
            bias = jnp.where(off <= BAND_R, -(slope * dil) * off.astype(F32), NEG_BIG)
            bias_scr[ATT_NE * w + e] = bias

    ones_rhs = jnp.ones((ATT_KB, LANES), BF16)

    def block(w, qb, kb, vb, out_idx, bias):
        qb, kb, vb = qb.astype(BF16), kb.astype(BF16), vb.astype(BF16)
        sc = lax.dot_general(qb, kb, (((1,), (1,)), ((), ())), preferred_element_type=F32)
        sc = sc + bias
        m = jnp.max(sc, axis=-1, keepdims=True)
        p = jnp.exp2(sc - m).astype(BF16)
        acc = jnp.dot(p, jnp.concatenate([vb, ones_rhs[:vb.shape[0]]], axis=1),
                      preferred_element_type=F32)
        acc_scr[w, out_idx, :] = acc[:, :HEAD_DIM]
        l_scr[w, out_idx, :] = acc[:, HEAD_DIM:]
        m_scr[w, out_idx, :] = jnp.broadcast_to(m, (ATT_QB, LANES))

    def key_start(q0, class_len, nk):
        return min(max(q0 - BAND_R, 0), class_len - nk)

    for i in range(s_len // ATT_QB):
        q0 = i * ATT_QB
        k0 = key_start(q0, s_len, ATT_KB)
        qi, ki = pl.ds(q0, ATT_QB), pl.ds(k0, ATT_KB)
        block(0, q_ref[qi, :], k_ref[ki, :], v_ref[ki, :], qi,
              bias_scr[(q0 - k0) // BAND_R])

    for c4 in range(ATT_C4):
        for i in range(cl // ATT_QB):
            q0 = i * ATT_QB
            k0 = key_start(q0, cl, ATT_KB)
            qi, ki = pl.ds(c4 * cl + q0, ATT_QB), pl.ds(c4 * cl + k0, ATT_KB)
            block(1, q4[qi, :], k4[ki, :], v4[ki, :], qi,
                  bias_scr[ATT_NE + (q0 - k0) // BAND_R])

    sub_len = s_len // DILATIONS[2]
    step4 = DILATIONS[2] // ATT_C4
    for c4 in range(ATT_C4):
        for c in range(step4):
            ki = pl.ds(c4 * cl + c, sub_len, stride=step4)
            for q0 in range(0, sub_len, ATT_QB):
                qi = pl.ds(c4 * cl + c + step4 * q0, ATT_QB, stride=step4)
                block(2, q4[qi, :], k4[ki, :], v4[ki, :], qi,
                      bias_scr[2 * ATT_NE + q0 // BAND_R][:, :sub_len])

    chunk = 256
    for c4 in range(ATT_C4):
        for r in range(cl // chunk):
            nat = pl.ds(c4 + ATT_C4 * r * chunk, chunk, stride=ATT_C4)
            grp = pl.ds(c4 * cl + r * chunk, chunk)
            m0, m1, m2 = m_scr[0, nat, :], m_scr[1, grp, :], m_scr[2, grp, :]
            mm = jnp.maximum(jnp.maximum(m0, m1), m2)
            a0, a1, a2 = jnp.exp2(m0 - mm), jnp.exp2(m1 - mm), jnp.exp2(m2 - mm)
            num = a0 * acc_scr[0, nat, :] + a1 * acc_scr[1, grp, :] + a2 * acc_scr[2, grp, :]
            den = a0 * l_scr[0, nat, :] + a1 * l_scr[1, grp, :] + a2 * l_scr[2, grp, :]
            out_scr[nat, :] = num / den
    o_ref[0] = out_scr[...].astype(o_ref.dtype)


def _attn_group(u4, w_cast):
    _, b, s, _ = u4.shape
    slopes = jnp.asarray([2.0 ** (-8.0 * (i + 1) / N_HEADS) for i in range(N_HEADS)], F32)
    head = lambda base: pl.BlockSpec((None, None, s, HEAD_DIM),
                                     lambda bi, hi: (base + hi, bi, 0, 0))
    cast_spec = _cast_spec(w_cast, (b, N_HEADS))
    return pl.pallas_call(
        _attn_kernel,
        grid=(b, N_HEADS),
        in_specs=[pl.BlockSpec(memory_space=pltpu.SMEM),
                  head(Q_SLAB0), head(Q_SLAB0 + N_HEADS), head(Q_SLAB0 + 2 * N_HEADS),
                  cast_spec],
        out_specs=[pl.BlockSpec((1, s, HEAD_DIM), lambda bi, hi: (bi, 0, hi)), cast_spec],
        out_shape=[jax.ShapeDtypeStruct((b, s, ATTN_WIDTH), BF16),
                   jax.ShapeDtypeStruct(w_cast.shape, BF16)],
        scratch_shapes=[pltpu.VMEM((s, HEAD_DIM), F32),
                        pltpu.VMEM((s, HEAD_DIM), F32),
                        pltpu.VMEM((s, HEAD_DIM), F32),
                        pltpu.VMEM((len(DILATIONS) * ATT_NE, ATT_QB, ATT_KB), F32),
                        pltpu.VMEM((3, s, HEAD_DIM), F32),
                        pltpu.VMEM((3, s, LANES), F32),
                        pltpu.VMEM((3, s, LANES), F32),
                        pltpu.VMEM((s, HEAD_DIM), F32)],
        compiler_params=_cparams(("parallel", "arbitrary")),
        name="attn_group",
    )(slopes, u4, u4, u4, w_cast)


OUT_NCOL = 512


def _out_proj_kernel(x_ref, y_ref, b_ref, w_ref, lg_ref, lb_ref, g_ref, o_ref, h_ref):
    ys =[y_ref[lb] for lb in range(CONV_LB)]
    tot = ys[0]
    for t in ys[1:]:
        tot = tot + t
    mu = jnp.sum(tot, axis=-1, keepdims=True) * (1.0 / CONV_CH)
    sq = None
    for t in ys:
        c = t - mu
        sq = c * c if sq is None else sq + c * c
    var = jnp.sum(sq, axis=-1, keepdims=True) * (1.0 / CONV_CH)
    rstd = lax.rsqrt(var + LN_EPS)
    a_cols = []
    for lb, t in enumerate(ys):
        ls = slice(lb * LANES, (lb + 1) * LANES)
        z = (t - mu) * rstd * lg_ref[:, ls] + lb_ref[:, ls]
        a_cols.append((z * jax.nn.sigmoid(z)).astype(BF16))
    a = jnp.concatenate(a_cols, axis=1)

    ss = None
    for c0 in range(0, D_MODEL, OUT_NCOL):
        cs = slice(c0, c0 + OUT_NCOL)
        part = jnp.dot(b_ref[...], w_ref[CONV_CH:, cs], preferred_element_type=F32)
        part = part + jnp.dot(a, w_ref[0:CONV_CH, cs], preferred_element_type=F32)
        x1 = x_ref[:, cs] + part
        o_ref[:, cs] = x1
        s = jnp.sum(x1 * x1, axis=-1, keepdims=True)
        ss = s if ss is None else ss + s
    rstd = lax.rsqrt(ss * (1.0 / D_MODEL) + RMS_EPS)
    h_ref[...] = (o_ref[...] * rstd * g_ref[...]).astype(BF16)


def _out_proj(x2, y_slabs, b2, w_bf, conv_ln_g, conv_ln_b, norm2_g, tm=512):
    m, d = x2.shape
    row = lambda shape: pl.BlockSpec(shape, lambda i: (0,) * len(shape))
    return pl.pallas_call(
        _out_proj_kernel,
        grid=(m // tm,),
        in_specs=[
            pl.BlockSpec((tm, d), lambda i: (i, 0)),
            pl.BlockSpec((CONV_LB, tm, LANES), lambda i: (0, i, 0)),
            pl.BlockSpec((tm, ATTN_WIDTH), lambda i: (i, 0)),
            row((d, d)), row((1, CONV_CH)), row((1, CONV_CH)), row((1, d)),
        ],
        out_specs=[pl.BlockSpec((tm, d), lambda i: (i, 0)),
                   pl.BlockSpec((tm, d), lambda i: (i, 0))],
        out_shape=[jax.ShapeDtypeStruct((m, d), F32),
                   jax.ShapeDtypeStruct((m, d), BF16)],
        compiler_params=_cparams(("parallel",)),
        name="out_proj",
    )(x2, y_slabs, b2, w_bf, conv_ln_g.reshape(1, CONV_CH), conv_ln_b.reshape(1, CONV_CH),
      norm2_g.reshape(1, d))


FFN_TM = 1024
FFN_TF = 512
FFN_PAD = BF16_ROWS
FFN_UOFF = 8
FFN_GROUP = 2
FFN_XC = 256
FFN_NXC = D_MODEL // FFN_XC


def _ffn_kernel(h_ref, hp_ref, hn_ref, x_ref, wg_ref, wv_ref, cwg_ref, cwv_ref,
                cbg_ref, cbv_ref, wd_ref, o_ref, h_scr, ug_scr, uv_scr, *, tiles_per_seq):
    i = pl.program_id(0)
    j = pl.program_id(1)
    tm = FFN_TM

    @pl.when(j == 0)
    def _():
        keep_prev = ((i % tiles_per_seq) != 0).astype(F32)
        keep_next = ((i % tiles_per_seq) != tiles_per_seq - 1).astype(F32)
        rid = lax.broadcasted_iota(jnp.int32, (FFN_PAD, 1), 0)
        hp = hp_ref[...].astype(F32)[FFN_PAD - 1:FFN_PAD, :] * keep_prev
        hn = hn_ref[...].astype(F32)[0:1, :] * keep_next
        h_scr[0:tm, :] = h_ref[...]
        h_scr[tm:, :] = jnp.where(rid == 0, hn, jnp.where(rid == 1, hp, 0.0)).astype(BF16)
        o_ref[...] = jnp.zeros_like(o_ref)

    for c in range(FFN_NXC):
        @pl.when(j == c)
        def _(c=c):
            o_ref[:, c * FFN_XC:(c + 1) * FFN_XC] += x_ref[...]

    base = FFN_UOFF

    def up_conv(w_ref, cw_ref, cb_ref, u_scr, c0, c1):
        u = jnp.dot(h_scr[...], w_ref[:, c0 * LANES:c1 * LANES], preferred_element_type=F32)
        cols = []
        for c in range(c0, c1):
            ls = slice(c * LANES, (c + 1) * LANES)
            us = slice((c - c0) * LANES, (c - c0 + 1) * LANES)
            u_scr[c, base:base + tm, :] = u[0:tm, us]
            u_scr[c, base + tm:base + tm + 1, :] = u[tm:tm + 1, us]
            u_scr[c, base - 1:base, :] = u[tm + 1:tm + 2, us]
            cols.append(cw_ref[0:1, ls] * u_scr[c, base - 1:base - 1 + tm, :]
                        + cw_ref[1:2, ls] * u_scr[c, base:base + tm, :]
                        + cw_ref[2:3, ls] * u_scr[c, base + 1:base + 1 + tm, :]
                        + cb_ref[:, ls])
        return jnp.concatenate(cols, axis=1)

    acts = []
    for c0 in range(0, FFN_TF // LANES, FFN_GROUP):
        gte = up_conv(wg_ref, cwg_ref, cbg_ref, ug_scr, c0, c0 + FFN_GROUP)
        val = up_conv(wv_ref, cwv_ref, cbv_ref, uv_scr, c0, c0 + FFN_GROUP)
        acts.append((gte * jax.nn.sigmoid(gte) * val).astype(BF16))
    act = jnp.concatenate(acts, axis=1)
    o_ref[...] += jnp.dot(act, wd_ref[...], preferred_element_type=F32)


def _conv_ffn(x1, h2, w_up_bf, ffn_dw_w, ffn_dw_b, w_down_bf, seq_len):
    m, d = x1.shape
    tm, tf = FFN_TM, FFN_TF
    nf = FFN_DIM // tf
    assert nf >= FFN_NXC
    hb = tm // FFN_PAD
    n_hblk = m // FFN_PAD
    kern = functools.partial(_ffn_kernel, tiles_per_seq=seq_len // tm)
    return pl.pallas_call(
        kern,
        grid=(m // tm, nf),
        in_specs=[
            pl.BlockSpec((tm, d), lambda i, j: (i, 0)),
            pl.BlockSpec((FFN_PAD, d), lambda i, j: (jnp.maximum(i * hb - 1, 0), 0)),
            pl.BlockSpec((FFN_PAD, d), lambda i, j: (jnp.minimum((i + 1) * hb, n_hblk - 1), 0)),
            pl.BlockSpec((tm, FFN_XC), lambda i, j: (i, jnp.minimum(j, FFN_NXC - 1))),
            pl.BlockSpec((d, tf), lambda i, j: (0, j)),
            pl.BlockSpec((d, tf), lambda i, j: (0, nf + j)),
            pl.BlockSpec((3, tf), lambda i, j: (0, j)),
            pl.BlockSpec((3, tf), lambda i, j: (0, nf + j)),
            pl.BlockSpec((1, tf), lambda i, j: (0, j)),
            pl.BlockSpec((1, tf), lambda i, j: (0, nf + j)),
            pl.BlockSpec((tf, d), lambda i, j: (j, 0)),
        ],
        out_specs=pl.BlockSpec((tm, d), lambda i, j: (i, 0)),
        out_shape=jax.ShapeDtypeStruct((m, d), F32),
        scratch_shapes=[pltpu.VMEM((tm + FFN_PAD, d), BF16),
                        pltpu.VMEM((tf // LANES, tm + 2 * FFN_UOFF, LANES), F32),
                        pltpu.VMEM((tf // LANES, tm + 2 * FFN_UOFF, LANES), F32)],
        compiler_params=_cparams(("parallel", "arbitrary")),
        name="conv_ffn",
    )(h2, h2, h2, x1, w_up_bf, w_up_bf, ffn_dw_w, ffn_dw_w,
      ffn_dw_b.reshape(1, -1), ffn_dw_b.reshape(1, -1), w_down_bf)


def kernel(x, norm1_g, w_in, conv_dw_w, conv_dw_b, conv_ln_g, conv_ln_b, q_norm_g, k_norm_g,
           w_out, norm2_g, w_up, ffn_dw_w, ffn_dw_b, w_down):
    b, s, d = x.shape
    x2 = x.reshape(b * s, d)
    u, w_out_bf = _in_proj(x2, norm1_g, w_in.astype(BF16), q_norm_g, k_norm_g, w_out)
    u4 = u.reshape(U_SLABS, b, s, LANES)
    y_slabs, w_down_bf = _conv_group(u4, conv_dw_w, conv_dw_b, w_down)
    b_out, w_up_bf = _attn_group(u4, w_up)
    x1, h2 = _out_proj(x2, y_slabs.reshape(CONV_LB, b * s, LANES),
                       b_out.reshape(b * s, ATTN_WIDTH), w_out_bf,
                       conv_ln_g, conv_ln_b, norm2_g)
    out = _conv_ffn(x1, h2, w_up_bf, ffn_dw_w, ffn_dw_b, w_down_bf, s)
    return out.reshape(b, s, d)
```
